```python
import math
import jax, jax.numpy as jnp
from jax import lax
import numpy as np

D_MODEL = 1024
BATCH = 8
SEQ = 2048
DEPTH = 1

HEAD_DIM = 64
ATTN_GROUPS = ((128, 1), (512, 4), (2048, 16))
HEADS_PER_GROUP = 4
N_ATTN_HEADS = HEADS_PER_GROUP * len(ATTN_GROUPS)
ATTN_WIDTH = N_ATTN_HEADS * HEAD_DIM
ATTN_OUT_WIDTH = HEADS_PER_GROUP * HEAD_DIM
CONV_WIDTH = D_MODEL
CONV_K = 3
N_BUCKETS = 32
MAX_DISTANCE = 2048
N_EXPERTS = 32
TOP_K = 4
D_EXPERT = D_MODEL
SWIGLU_LIMIT = 7.0
SWIGLU_ALPHA = 1.702
MOE_BLOCK = 128
EPS = 1e-6
IN_SPLITS = (CONV_WIDTH, CONV_WIDTH, CONV_WIDTH, ATTN_WIDTH, ATTN_WIDTH, ATTN_WIDTH, D_MODEL, D_MODEL)
IN_WIDTH = sum(IN_SPLITS)
IN_CUTS = tuple(int(c) for c in np.cumsum(IN_SPLITS)[:-1])

kernel_name = "hybrid_gated_conv_dilated_attn_moe"


def rms_norm(x, g):
    xf = x.astype(jnp.float32)
    y = xf * lax.rsqrt(jnp.mean(xf * xf, axis=-1, keepdims=True) + EPS)
    return (y * g.astype(jnp.float32)).astype(x.dtype)


def t5_causal_bucket(dist):
    max_exact = N_BUCKETS // 2
    d = jnp.maximum(dist.astype(jnp.float32), 1.0)
    large = max_exact + (jnp.log(d / max_exact) / math.log(MAX_DISTANCE / max_exact)
                         * (N_BUCKETS - max_exact)).astype(jnp.int32)
    large = jnp.minimum(large, N_BUCKETS - 1)
    return jnp.where(dist < max_exact, dist, large)


def dilated_window_attention(q, k, v, rel_bias_g, window, dil):
    B, S, H, Dh = q.shape
    blk = window // dil
    L = -(-S // (dil * blk)) * blk
    pad = L * dil - S
    nblk = L // blk

    def to_blocks(t):
        t = jnp.pad(t, ((0, 0), (0, pad), (0, 0), (0, 0))).reshape(B, L, dil, H, Dh)
        return t.transpose(0, 2, 3, 1, 4).reshape(B, dil, H, nblk, blk, Dh)

    def with_prev(t):
        prev = jnp.pad(t, ((0, 0), (0, 0), (0, 0), (1, 0), (0, 0), (0, 0)))[:, :, :, :-1]
        return jnp.concatenate([prev, t], axis=4)

    qb = to_blocks(q)
    kw = with_prev(to_blocks(k))
    vw = with_prev(to_blocks(v))

    qi = jnp.arange(blk)[:, None]
    kj = jnp.arange(2 * blk)[None, :]
    delta = qi + blk - kj
    band = (delta >= 0) & (delta <= blk)
    valid = band[None] & ((jnp.arange(nblk)[:, None, None] > 0) | (kj[None] >= blk))
    bucket = t5_causal_bucket(jnp.clip(delta, 0, blk) * dil)
    bias = rel_bias_g[bucket].transpose(2, 0, 1).astype(jnp.float32)

    logits = jnp.einsum('brhnqd,brhnkd->brhnqk', qb, kw,
                        preferred_element_type=jnp.float32) * (Dh ** -0.5)
    logits = logits + bias[:, None]
    logits = jnp.where(valid, logits, -jnp.inf)
    m = jnp.max(logits, axis=-1, keepdims=True)
    p = jnp.exp(logits - m)
    s = jnp.sum(p, axis=-1, keepdims=True)
    o = jnp.einsum('brhnqk,brhnkd->brhnqd', p, vw.astype(jnp.float32)) / s
    lse = (m + jnp.log(s))[..., 0]

    o = o.reshape(B, dil, H, L, Dh).transpose(0, 3, 1, 2, 4).reshape(B, L * dil, H, Dh)[:, :S]
    lse = lse.reshape(B, dil, H, L).transpose(0, 3, 1, 2).reshape(B, L * dil, H)[:, :S]
    return o, lse


def token_mixer(h, w_in, q_norm_g, k_norm_g, rel_bias, conv_w, w_branch_conv, w_branch_attn, w_out):
    B, S, _ = h.shape
    proj = h @ w_in
    c_b, c_c, c_x, q, k, v, g_conv, g_attn = jnp.split(proj, IN_CUTS, axis=-1)

    u = c_c * c_x
    u = lax.conv_general_dilated(u, conv_w.reshape(CONV_K, 1, CONV_WIDTH).astype(u.dtype),
                                 window_strides=(1,), padding=[(CONV_K - 1, 0)],
                                 dimension_numbers=('NWC', 'WIO', 'NWC'),
                                 feature_group_count=CONV_WIDTH)
    y_conv = (c_b * u) @ w_branch_conv

    q = rms_norm(q.reshape(B, S, N_ATTN_HEADS, HEAD_DIM), q_norm_g)
    k = rms_norm(k.reshape(B, S, N_ATTN_HEADS, HEAD_DIM), k_norm_g)
    v = v.reshape(B, S, N_ATTN_HEADS, HEAD_DIM)
    outs, lses = [], []
    for gi, (window, dil) in enumerate(ATTN_GROUPS):
        hs = slice(gi * HEADS_PER_GROUP, (gi + 1) * HEADS_PER_GROUP)
        o_g, lse_g = dilated_window_attention(q[:, :, hs], k[:, :, hs], v[:, :, hs],
                                              rel_bias[:, hs], window, dil)
        outs.append(o_g)
        lses.append(lse_g)
    outs = jnp.stack(outs, axis=0)
    wts = jax.nn.softmax(jnp.stack(lses, axis=0), axis=0)
    o = jnp.sum(wts[..., None] * outs, axis=0).astype(h.dtype)
    y_attn = o.reshape(B, S, ATTN_OUT_WIDTH) @ w_branch_attn

    merged = jax.nn.sigmoid(g_conv) * y_conv + jax.nn.sigmoid(g_attn) * y_attn
    return merged @ w_out


def moe(h, w_router, b_router, w_gate_up, b_gate_up, w_down, b_down):
    B, S, D = h.shape
    T = B * S
    TK = T * TOP_K
    hf = h.reshape(T, D)
    logits = (hf @ w_router + b_router).astype(jnp.float32)
    top_vals, top_idx = lax.top_k(logits, TOP_K)
    gates = jax.nn.softmax(top_vals, axis=-1).astype(h.dtype)

    flat_e = top_idx.reshape(-1)
    flat_tok = jnp.arange(TK, dtype=jnp.int32) // TOP_K
    order = jnp.argsort(flat_e, stable=True)
    sorted_e = flat_e[order]
    sorted_tok = flat_tok[order]
    sorted_w = gates.reshape(-1)[order]

    counts = jax.ops.segment_sum(jnp.ones((TK,), jnp.int32), flat_e, num_segments=N_EXPERTS)
    start = jnp.cumsum(counts) - counts
    pcounts = (counts + MOE_BLOCK - 1) // MOE_BLOCK * MOE_BLOCK
    pend = jnp.cumsum(pcounts)
    poff = pend - pcounts
    dest = poff[sorted_e] + (jnp.arange(TK, dtype=jnp.int32) - start[sorted_e])

    p_max = TK + N_EXPERTS * MOE_BLOCK
    n_blocks = p_max // MOE_BLOCK
    row_tok = jnp.full((p_max,), T, jnp.int32).at[dest].set(sorted_tok)
    hp = jnp.concatenate([hf, jnp.zeros((1, D), hf.dtype)], axis=0)
    xs = hp[row_tok].reshape(n_blocks, MOE_BLOCK, D)
    block_e = jnp.clip(jnp.searchsorted(pend, jnp.arange(n_blocks) * MOE_BLOCK, side='right'),
                       0, N_EXPERTS - 1)

    def expert_block(args):
        xb, e = args
        gu = xb @ w_gate_up[e] + b_gate_up[e]
        gate, up = gu[:, :D_EXPERT], gu[:, D_EXPERT:]
        gate = jnp.minimum(gate, SWIGLU_LIMIT)
        up = jnp.clip(up, -SWIGLU_LIMIT, SWIGLU_LIMIT)
        act = (up + 1) * (gate * jax.nn.sigmoid(SWIGLU_ALPHA * gate))
        return act @ w_down[e] + b_down[e]

    ys = lax.map(expert_block, (xs, block_e)).reshape(p_max, D)
    y_assign = ys[dest] * sorted_w[:, None]
    out = jax.ops.segment_sum(y_assign, sorted_tok, num_segments=T)
    return out.reshape(B, S, D)


def setup_inputs(seed: int = 0) -> dict:
    key = jax.random.key(seed)
    ks = jax.random.split(key, 20)
    f32 = jnp.float32
    nrm = lambda k, shape, scale: jax.random.normal(k, shape, f32) * scale
    return {
        "x": nrm(ks[0], (BATCH, SEQ, D_MODEL), 1.0),
        "norm1_g": 1.0 + nrm(ks[1], (DEPTH, D_MODEL), 0.01),
        "w_in": nrm(ks[2], (DEPTH, D_MODEL, IN_WIDTH), D_MODEL ** -0.5),
        "q_norm_g": 1.0 + nrm(ks[3], (DEPTH, HEAD_DIM), 0.01),
        "k_norm_g": 1.0 + nrm(ks[4], (DEPTH, HEAD_DIM), 0.01),
        "rel_bias": nrm(ks[5], (N_BUCKETS, N_ATTN_HEADS), 0.1),
        "conv_w": nrm(ks[6], (DEPTH, CONV_K, CONV_WIDTH), CONV_K ** -0.5),
        "w_branch_conv": nrm(ks[7], (DEPTH, CONV_WIDTH, D_MODEL), CONV_WIDTH ** -0.5),
        "w_branch_attn": nrm(ks[8], (DEPTH, ATTN_OUT_WIDTH, D_MODEL), ATTN_OUT_WIDTH ** -0.5),
        "w_out": nrm(ks[9], (DEPTH, D_MODEL, D_MODEL), D_MODEL ** -0.5),
        "norm2_g": 1.0 + nrm(ks[10], (DEPTH, D_MODEL), 0.01),
        "w_router": nrm(ks[11], (DEPTH, D_MODEL, N_EXPERTS), D_MODEL ** -0.5),
        "b_router": nrm(ks[12], (DEPTH, N_EXPERTS), 0.01),
        "w_gate_up": nrm(ks[13], (DEPTH, N_EXPERTS, D_MODEL, 2 * D_EXPERT), D_MODEL ** -0.5),
        "b_gate_up": nrm(ks[14], (DEPTH, N_EXPERTS, 2 * D_EXPERT), 0.01),
        "w_down": nrm(ks[15], (DEPTH, N_EXPERTS, D_EXPERT, D_MODEL), D_EXPERT ** -0.5),
        "b_down": nrm(ks[16], (DEPTH, N_EXPERTS, D_MODEL), 0.01),
    }


def reference(x, norm1_g, w_in, q_norm_g, k_norm_g, rel_bias, conv_w, w_branch_conv,
              w_branch_attn, w_out, norm2_g, w_router, b_router, w_gate_up, b_gate_up,
              w_down, b_down):
    for l in range(DEPTH):
        h = rms_norm(x, norm1_g[l])
        x = x + token_mixer(h, w_in[l], q_norm_g[l], k_norm_g[l], rel_bias, conv_w[l],
                            w_branch_conv[l], w_branch_attn[l], w_out[l])
        h = rms_norm(x, norm2_g[l])
        x = x + moe(h, w_router[l], b_router[l], w_gate_up[l], b_gate_up[l],
                    w_down[l], b_down[l])
    return x
```

```python
import functools
import math

import numpy as np
import jax
import jax.numpy as jnp
from jax import lax
from jax.experimental import pallas as pl
from jax.experimental.pallas import tpu as pltpu

F32 = jnp.float32
BF16 = jnp.bfloat16
I32 = jnp.int32

HEAD_DIM = 64
ATTN_GROUPS = ((128, 1), (512, 4), (2048, 16))
HEADS_PER_GROUP = 4
N_GROUPS = len(ATTN_GROUPS)
ATTN_BLK = 128
N_BUCKETS = 32
MAX_DISTANCE = 2048
CONV_K = 3
N_EXPERTS = 32
TOP_K = 4
SWIGLU_LIMIT = 7.0
SWIGLU_ALPHA = 1.702
EPS = 1e-6
MASK_VALUE = -1e30

LANES = 128
VMEM_LIMIT = 52 * 1024 * 1024

MOE_BM = 256
ROW_TILE = 256


def _rms(x, gain):
    return x * lax.rsqrt(jnp.mean(x * x, axis=-1, keepdims=True) + EPS) * gain


def _dot(a, b):
    return jnp.dot(a, b, preferred_element_type=F32)


def _dot_nt(a, b):
    return lax.dot_general(a, b, (((1,), (1,)), ((), ())), preferred_element_type=F32)


def _inproj_kernel(x_ref, g_ref, w_ref, o_ref):
    h = _rms(x_ref[...], g_ref[...]).astype(BF16)
    o_ref[...] = _dot(h, w_ref[...]).astype(BF16)


def _inproj(x2, g1, w_in_bf, tm, tn):
    T, D = x2.shape
    N = w_in_bf.shape[1]
    return pl.pallas_call(
        _inproj_kernel,
        grid=(N // tn, T // tm),
        in_specs=[
            pl.BlockSpec((tm, D), lambda j, i: (i, 0)),
            pl.BlockSpec((1, D), lambda j, i: (0, 0)),
            pl.BlockSpec((D, tn), lambda j, i: (0, j)),
        ],
        out_specs=pl.BlockSpec((tm, tn), lambda j, i: (i, j)),
        out_shape=jax.ShapeDtypeStruct((T, N), BF16),
        compiler_params=pltpu.CompilerParams(
            dimension_semantics=("arbitrary", "arbitrary"), vmem_limit_bytes=VMEM_LIMIT),
        name="inproj",
    )(x2, g1, w_in_bf)


def _t5_bucket_np(dist):
    max_exact = N_BUCKETS // 2
    d = np.maximum(dist.astype(np.float64), 1.0)
    large = max_exact + (np.log(d / max_exact) / math.log(MAX_DISTANCE / max_exact)
                         * (N_BUCKETS - max_exact)).astype(np.int32)
    large = np.minimum(large, N_BUCKETS - 1)
    return np.where(dist < max_exact, dist, large)


def _attn_bias_tables(rel_bias):
    blk = ATTN_BLK
    qi = np.arange(blk)[:, None]
    kj = np.arange(2 * blk)[None, :]
    delta = qi + blk - kj
    band = (delta >= 0) & (delta <= blk)
    first = band & (kj >= blk)
    tables = []
    for gi, (_, dil) in enumerate(ATTN_GROUPS):
        bucket = _t5_bucket_np(np.clip(delta, 0, blk) * dil)
        hs = slice(gi * HEADS_PER_GROUP, (gi + 1) * HEADS_PER_GROUP)
        bias = rel_bias[:, hs][bucket].transpose(2, 0, 1).astype(F32)
        rest = jnp.where(band[None], bias, MASK_VALUE)
        frst = jnp.where(first[None], bias, MASK_VALUE)
        tables.append(jnp.stack([rest, frst], axis=1))
    return jnp.stack(tables, axis=0)


def _attn_kernel(q0, q1, q2, k0, k1, k2, v0, v1, v2, gq_ref, gk_ref, bias_ref, o_ref,
                 qs, ks, vs, acc_s, m_s, l_s):
    S = q0.shape[0]
    pad = ks.shape[0] - S
    blk = ATTN_BLK
    hd = HEAD_DIM
    q_refs, k_refs, v_refs = (q0, q1, q2), (k0, k1, k2), (v0, v1, v2)

    ks[0:pad, :] = jnp.zeros((pad, LANES), F32)
    vs[0:pad, :] = jnp.zeros((pad, LANES), F32)

    def head_norm(x, gain):
        halves = []
        for hh in range(2):
            xh = x[:, hh * hd:(hh + 1) * hd]
            halves.append(xh * lax.rsqrt(jnp.mean(xh * xh, axis=-1, keepdims=True) + EPS))
        return jnp.concatenate(halves, axis=-1) * gain

    for g, (window, dil) in enumerate(ATTN_GROUPS):
        span = blk * dil
        nblk = S // span
        stride = dil if dil > 1 else None
        qs[...] = head_norm(q_refs[g][...].astype(F32), gq_ref[...]) * (hd ** -0.5)
        ks[pad:pad + S, :] = head_norm(k_refs[g][...].astype(F32), gk_ref[...])
        vs[pad:pad + S, :] = v_refs[g][...].astype(F32)

        def body(idx, carry, g=g, dil=dil, span=span, nblk=nblk, stride=stride):
            r = idx // nblk
            n = idx % nblk
            qstart = r + n * span
            wstart = pad + qstart - span
            qb = qs[pl.ds(qstart, blk, stride=stride), :]
            kw = ks[pl.ds(wstart, 2 * blk, stride=stride), :]
            vw = vs[pl.ds(wstart, 2 * blk, stride=stride), :]
            first = jnp.where(n == 0, 1, 0)
            accs, ms, ls = [], [], []
            for hh in range(2):
                sl = slice(hh * hd, (hh + 1) * hd)
                s = _dot_nt(qb[:, sl].astype(BF16), kw[:, sl].astype(BF16))
                s = s + bias_ref[g, hh, first]
                m = jnp.max(s, axis=-1, keepdims=True)
                p = jnp.exp(s - m)
                l = jnp.sum(p, axis=-1, keepdims=True)
                accs.append(_dot(p.astype(BF16), vw[:, sl].astype(BF16)))
                ms.append(jnp.broadcast_to(m, (blk, hd)))
                ls.append(jnp.broadcast_to(l, (blk, hd)))
            rows = pl.ds(qstart, blk, stride=stride)
            acc_s[g, rows, :] = jnp.concatenate(accs, axis=-1)
            m_s[g, rows, :] = jnp.concatenate(ms, axis=-1)
            l_s[g, rows, :] = jnp.concatenate(ls, axis=-1)
            return carry

        lax.fori_loop(0, S // blk, body, 0)

    m_all = jnp.maximum(jnp.maximum(m_s[0], m_s[1]), m_s[2])
    num = jnp.zeros((S, LANES), F32)
    den = jnp.zeros((S, LANES), F32)
    for g in range(N_GROUPS):
        w = jnp.exp(m_s[g] - m_all)
        num = num + w * acc_s[g]
        den = den + w * l_s[g]
    o_ref[...] = (num / den).astype(BF16)


def _attention(proj, gq2, gk2, bias_tab, B, S, q_col, k_col, v_col):
    T = proj.shape[0]
    n_pairs = HEADS_PER_GROUP // 2

    def col_spec(base, g):
        return pl.BlockSpec((S, LANES), lambda b, p, base=base, g=g: (b, base + n_pairs * g + p))

    in_specs = ([col_spec(q_col, g) for g in range(N_GROUPS)]
                + [col_spec(k_col, g) for g in range(N_GROUPS)]
                + [col_spec(v_col, g) for g in range(N_GROUPS)]
                + [pl.BlockSpec((1, LANES), lambda b, p: (0, 0)),
                   pl.BlockSpec((1, LANES), lambda b, p: (0, 0)),
                   pl.BlockSpec((N_GROUPS, 2, 2, ATTN_BLK, 2 * ATTN_BLK), lambda b, p: (0, p, 0, 0, 0))])
    pad = ATTN_BLK * max(d for _, d in ATTN_GROUPS)
    return pl.pallas_call(
        _attn_kernel,
        grid=(B, n_pairs),
        in_specs=in_specs,
        out_specs=pl.BlockSpec((S, LANES), lambda b, p: (b, p)),
        out_shape=jax.ShapeDtypeStruct((T, HEADS_PER_GROUP * HEAD_DIM), BF16),
        scratch_shapes=[
            pltpu.VMEM((S, LANES), F32),
            pltpu.VMEM((pad + S, LANES), F32),
            pltpu.VMEM((pad + S, LANES), F32),
            pltpu.VMEM((N_GROUPS, S, LANES), F32),
            pltpu.VMEM((N_GROUPS, S, LANES), F32),
            pltpu.VMEM((N_GROUPS, S, LANES), F32),
        ],
        compiler_params=pltpu.CompilerParams(
            dimension_semantics=("arbitrary", "arbitrary"), vmem_limit_bytes=VMEM_LIMIT),
        name="attn",
    )(*([proj] * 9), gq2, gk2, bias_tab)


def _post_kernel(cb_ref, cc_ref, cx_ref, gc_ref, ga_ref, ccp_ref, cxp_ref, o_ref, x_ref, cw_ref,
                 wbc_ref, wba_ref, wout_ref, g2_ref, wrh_ref, wrl_ref, br_ref,
                 x1_ref, h2_ref, idx_ref, gate_ref, *, seq_len):
    i = pl.program_id(0)
    tm = x_ref.shape[0]
    u = cc_ref[...].astype(F32) * cx_ref[...].astype(F32)
    prev = ccp_ref[...].astype(F32) * cxp_ref[...].astype(F32)
    prev = jnp.where((i * tm) % seq_len == 0, 0.0, prev)
    last = prev.shape[0] - 1
    p1 = prev[last:last + 1, :]
    p2 = prev[last - 1:last, :]
    rows = lax.broadcasted_iota(I32, (tm, 1), 0)
    u1 = jnp.where(rows == 0, p1, pltpu.roll(u, 1, axis=0))
    u2 = jnp.where(rows == 0, p2, jnp.where(rows == 1, p1, pltpu.roll(u, 2, axis=0)))
    conv = cw_ref[0:1, :] * u2 + cw_ref[1:2, :] * u1 + cw_ref[2:3, :] * u
    y_conv = _dot((cb_ref[...].astype(F32) * conv).astype(BF16), wbc_ref[...])
    y_attn = _dot(o_ref[...], wba_ref[...])
    merged = (jax.nn.sigmoid(gc_ref[...].astype(F32)) * y_conv
              + jax.nn.sigmoid(ga_ref[...].astype(F32)) * y_attn)
    x1 = x_ref[...] + _dot(merged.astype(BF16), wout_ref[...])
    x1_ref[...] = x1
    h2 = _rms(x1, g2_ref[...])
    h2_ref[...] = h2

    h_hi = h2.astype(BF16)
    h_lo = (h2 - h_hi.astype(F32)).astype(BF16)
    logits = (_dot_nt(wrh_ref[...], h_hi) + _dot_nt(wrh_ref[...], h_lo)
              + _dot_nt(wrl_ref[...], h_hi)) + br_ref[...]
    e_iota = lax.broadcasted_iota(I32, logits.shape, 0)
    work = logits
    vals, idxs = [], []
    for _ in range(TOP_K):
        mx = jnp.max(work, axis=0, keepdims=True)
        ix = jnp.min(jnp.where(work == mx, e_iota, N_EXPERTS), axis=0, keepdims=True)
        vals.append(mx)
        idxs.append(ix)
        work = jnp.where(e_iota == ix, -jnp.inf, work)
    ex = [jnp.exp(v - vals[0]) for v in vals]
    tot = ex[0] + ex[1] + ex[2] + ex[3]
    idx_ref[...] = jnp.concatenate(idxs, axis=0)
    gate_ref[...] = jnp.concatenate([e / tot for e in ex], axis=0)


def _post(proj, o_attn, x2, conv_w, wbc, wba, wout, g2, wr_hi, wr_lo, br, tm, seq_len):
    T, D = x2.shape
    prev_rows = 16
    kern = functools.partial(_post_kernel, seq_len=seq_len)

    def colblk(c):
        return pl.BlockSpec((tm, D), lambda i, c=c: (i, c))

    def prevblk(c):
        return pl.BlockSpec((prev_rows, D),
                            lambda i, c=c: (jnp.maximum(i * (tm // prev_rows) - 1, 0), c))

    def whole(a):
        return pl.BlockSpec(a.shape, lambda i, nd=a.ndim: (0,) * nd)

    return pl.pallas_call(
        kern,
        grid=(T // tm,),
        in_specs=[colblk(0), colblk(1), colblk(2), colblk(3), colblk(4), prevblk(1), prevblk(2),
                  pl.BlockSpec((tm, o_attn.shape[1]), lambda i: (i, 0)),
                  pl.BlockSpec((tm, D), lambda i: (i, 0)),
                  whole(conv_w), whole(wbc), whole(wba), whole(wout), whole(g2),
                  whole(wr_hi), whole(wr_lo), whole(br)],
        out_specs=[pl.BlockSpec((tm, D), lambda i: (i, 0)),
                   pl.BlockSpec((tm, D), lambda i: (i, 0)),
                   pl.BlockSpec((TOP_K, tm), lambda i: (0, i)),
                   pl.BlockSpec((TOP_K, tm), lambda i: (0, i))],
        out_shape=[jax.ShapeDtypeStruct((T, D), F32),
                   jax.ShapeDtypeStruct((T, D), F32),
                   jax.ShapeDtypeStruct((TOP_K, T), I32),
                   jax.ShapeDtypeStruct((TOP_K, T), F32)],
        compiler_params=pltpu.CompilerParams(
            dimension_semantics=("arbitrary",), vmem_limit_bytes=VMEM_LIMIT),
        name="post",
    )(proj, proj, proj, proj, proj, proj, proj, o_attn, x2, conv_w, wbc, wba, wout, g2,
      wr_hi, wr_lo, br)


def _route_kernel(idx_ref, dest_ref, be_ref, nused_ref, rank_s, *, bm, chunk):
    T = idx_ref.shape[1]
    e_iota = lax.broadcasted_iota(I32, (N_EXPERTS, chunk), 0)
    upper = (lax.broadcasted_iota(I32, (chunk, chunk), 0)
             < lax.broadcasted_iota(I32, (chunk, chunk), 1)).astype(BF16)

    def onehots(c):
        off = pl.multiple_of(c * chunk, chunk)
        idc = idx_ref[:, pl.ds(off, chunk)]
        return off, [e_iota == idc[k:k + 1, :] for k in range(TOP_K)]

    def pass1(c, carry):
        off, oh = onehots(c)
        member = oh[0].astype(F32) + oh[1].astype(F32) + oh[2].astype(F32) + oh[3].astype(F32)
        before = _dot(member.astype(BF16), upper) + carry
        ranks = [jnp.sum(jnp.where(o, before, 0.0), axis=0, keepdims=True) for o in oh]
        rank_s[:, pl.ds(off, chunk)] = jnp.concatenate(ranks, axis=0)
        return carry + jnp.sum(member, axis=1, keepdims=True)

    counts = lax.fori_loop(0, T // chunk, pass1, jnp.zeros((N_EXPERTS, 1), F32))

    nb = jnp.floor((counts + (bm - 1)) * (1.0 / bm))
    lower = (lax.broadcasted_iota(I32, (N_EXPERTS, N_EXPERTS), 1)
             < lax.broadcasted_iota(I32, (N_EXPERTS, N_EXPERTS), 0)).astype(BF16)
    boff = _dot(lower, jnp.broadcast_to(nb, (N_EXPERTS, LANES)).astype(BF16))[:, 0:1]
    poff = boff * bm
    bend = boff + nb

    def pass2(c, carry):
        off, oh = onehots(c)
        offs = [jnp.sum(jnp.where(o, poff, 0.0), axis=0, keepdims=True) for o in oh]
        dest = jnp.concatenate(offs, axis=0) + rank_s[:, pl.ds(off, chunk)]
        dest_ref[:, pl.ds(off, chunk)] = dest.astype(I32)
        return carry

    lax.fori_loop(0, T // chunk, pass2, 0)

    nbp = be_ref.shape[1]
    b_iota = lax.broadcasted_iota(I32, (N_EXPERTS, nbp), 1).astype(F32)
    be = jnp.sum((bend <= b_iota).astype(F32), axis=0, keepdims=True)
    be_ref[...] = jnp.minimum(be, N_EXPERTS - 1).astype(I32)
    nused_ref[...] = jnp.broadcast_to(bend[N_EXPERTS - 1:N_EXPERTS, :], nused_ref.shape).astype(I32)


def _route(idx_t, bm, nblk_pad):
    K, T = idx_t.shape
    kern = functools.partial(_route_kernel, bm=bm, chunk=512)
    return pl.pallas_call(
        kern,
        out_shape=[jax.ShapeDtypeStruct((K, T), I32),
                   jax.ShapeDtypeStruct((1, nblk_pad), I32),
                   jax.ShapeDtypeStruct((1, LANES), I32)],
        scratch_shapes=[pltpu.VMEM((K, T), F32)],
        compiler_params=pltpu.CompilerParams(vmem_limit_bytes=VMEM_LIMIT),
        name="route",
    )(idx_t)


def _row_copy(src_ref, src_row, dst_ref, dst_row, sem):
    return pltpu.make_async_copy(src_ref.at[pl.ds(src_row, 1), :], dst_ref.at[pl.ds(dst_row, 1), :], sem)


def _dispatch_kernel(dest_ref, h_ref, xs_in_ref, xs_ref, sem):
    del xs_in_ref
    tt = h_ref.shape[0]

    def issue(t, carry):
        for k in range(TOP_K):
            _row_copy(h_ref, t, xs_ref, dest_ref[k, t], sem).start()
        return carry

    lax.fori_loop(0, tt, issue, 0)

    def drain(t, carry):
        for k in range(TOP_K):
            _row_copy(h_ref, t, xs_ref, dest_ref[k, t], sem).wait()
        return carry

    lax.fori_loop(0, tt, drain, 0)


def _dispatch(dest3, h2, xs_zero):
    nt, K, tt = dest3.shape
    T, D = h2.shape
    return pl.pallas_call(
        _dispatch_kernel,
        grid=(nt,),
        in_specs=[pl.BlockSpec((None, K, tt), lambda i: (i, 0, 0), memory_space=pltpu.SMEM),
                  pl.BlockSpec((tt, D), lambda i: (i, 0)),
                  pl.BlockSpec(memory_space=pl.ANY)],
        out_specs=pl.BlockSpec(memory_space=pl.ANY),
        out_shape=jax.ShapeDtypeStruct(xs_zero.shape, xs_zero.dtype),
        scratch_shapes=[pltpu.SemaphoreType.DMA(())],
        input_output_aliases={2: 0},
        compiler_params=pltpu.CompilerParams(
            dimension_semantics=("arbitrary",), vmem_limit_bytes=VMEM_LIMIT),
        name="dispatch",
    )(dest3, h2, xs_zero)


def _expert_kernel(be_ref, nu_ref, xs_ref, wgu_ref, bgu_ref, wd_ref, bd_ref, ys_ref, *, chunk):
    del be_ref
    d_exp = wd_ref.shape[0]

    @pl.when(pl.program_id(0) < nu_ref[0])
    def _():
        x = xs_ref[...].astype(BF16)
        y = jnp.zeros(ys_ref.shape, F32)
        for c in range(d_exp // chunk):
            lo = c * chunk
            gate = _dot(x, wgu_ref[:, lo:lo + chunk]) + bgu_ref[:, lo:lo + chunk]
            up = _dot(x, wgu_ref[:, d_exp + lo:d_exp + lo + chunk]) + bgu_ref[:, d_exp + lo:d_exp + lo + chunk]
            gate = jnp.minimum(gate, SWIGLU_LIMIT)
            up = jnp.clip(up, -SWIGLU_LIMIT, SWIGLU_LIMIT)
            act = (up + 1.0) * (gate * jax.nn.sigmoid(SWIGLU_ALPHA * gate))
            y = y + _dot(act.astype(BF16), wd_ref[lo:lo + chunk, :])
        ys_ref[...] = y + bd_ref[...]

    @pl.when(pl.program_id(0) >= nu_ref[0])
    def _():
        ys_ref[...] = jnp.zeros(ys_ref.shape, F32)


def _experts(be, nused, xs, wgu, bgu, wd, bd, bm):
    P, D = xs.shape
    E, _, two_de = wgu.shape
    d_exp = wd.shape[1]
    kern = functools.partial(_expert_kernel, chunk=256)

    def rows(b, be_r, nu_r):
        return (jnp.minimum(b, nu_r[0] - 1), 0)

    def per_expert(b, be_r, nu_r):
        return (be_r[b], 0, 0)

    grid_spec = pltpu.PrefetchScalarGridSpec(
        num_scalar_prefetch=2,
        grid=(P // bm,),
        in_specs=[pl.BlockSpec((bm, D), rows),
                  pl.BlockSpec((None, D, two_de), per_expert),
                  pl.BlockSpec((None, 1, two_de), per_expert),
                  pl.BlockSpec((None, d_exp, D), per_expert),
                  pl.BlockSpec((None, 1, D), per_expert)],
        out_specs=pl.BlockSpec((bm, D), lambda b, be_r, nu_r: (b, 0)),
    )
    return pl.pallas_call(
        kern,
        grid_spec=grid_spec,
        out_shape=jax.ShapeDtypeStruct((P, D), F32),
        compiler_params=pltpu.CompilerParams(
            dimension_semantics=("arbitrary",), vmem_limit_bytes=VMEM_LIMIT),
        name="experts",
    )(be, nused, xs, wgu, bgu, wd, bd)


def _combine_kernel(dest_ref, x1_ref, gates_ref, ys_ref, out_ref, buf, sem):
    tt = x1_ref.shape[0]

    def issue(t, carry):
        for k in range(TOP_K):
            _row_copy(ys_ref, dest_ref[k, t], buf.at[k], t, sem).start()
        return carry

    lax.fori_loop(0, tt, issue, 0)

    def drain(t, carry):
        for k in range(TOP_K):
            _row_copy(ys_ref, dest_ref[k, t], buf.at[k], t, sem).wait()
        return carry

    lax.fori_loop(0, tt, drain, 0)

    acc = x1_ref[...]
    for k in range(TOP_K):
        acc = acc + gates_ref[:, k:k + 1] * buf[k]
    out_ref[...] = acc


def _combine(dest3, x1, gates, ys):
    nt, K, tt = dest3.shape
    T, D = x1.shape
    return pl.pallas_call(
        _combine_kernel,
        grid=(nt,),
        in_specs=[pl.BlockSpec((None, K, tt), lambda i: (i, 0, 0), memory_space=pltpu.SMEM),
                  pl.BlockSpec((tt, D), lambda i: (i, 0)),
                  pl.BlockSpec((tt, K), lambda i: (i, 0)),
                  pl.BlockSpec(memory_space=pl.ANY)],
        out_specs=pl.BlockSpec((tt, D), lambda i: (i, 0)),
        out_shape=jax.ShapeDtypeStruct((T, D), F32),
        scratch_shapes=[pltpu.VMEM((K, tt, D), F32), pltpu.SemaphoreType.DMA(())],
        compiler_params=pltpu.CompilerParams(
            dimension_semantics=("arbitrary",), vmem_limit_bytes=VMEM_LIMIT),
        name="combine",
    )(dest3, x1, gates, ys)


def _layer(x2, B, S, norm1_g, w_in, q_norm_g, k_norm_g, rel_bias, conv_w, w_branch_conv,
           w_branch_attn, w_out, norm2_g, w_router, b_router, w_gate_up, b_gate_up, w_down, b_down):
    T, D = x2.shape
    attn_w = N_GROUPS * HEADS_PER_GROUP * HEAD_DIM
    cuts = np.cumsum([0, D, D, D, attn_w, attn_w, attn_w, D, D])
    order = [0, 1, 2, 6, 7, 3, 4, 5]
    w_in_r = jnp.concatenate([w_in[:, cuts[s]:cuts[s + 1]] for s in order], axis=1).astype(BF16)
    q_col = 5 * D // LANES
    k_col = q_col + attn_w // LANES
    v_col = k_col + attn_w // LANES

    proj = _inproj(x2, norm1_g.reshape(1, D), w_in_r, tm=512, tn=w_in_r.shape[1] // 2)

    gq2 = jnp.tile(q_norm_g.reshape(1, HEAD_DIM), (1, LANES // HEAD_DIM))
    gk2 = jnp.tile(k_norm_g.reshape(1, HEAD_DIM), (1, LANES // HEAD_DIM))
    o_attn = _attention(proj, gq2, gk2, _attn_bias_tables(rel_bias), B, S, q_col, k_col, v_col)

    wr_hi = w_router.T.astype(BF16)
    wr_lo = (w_router.T - wr_hi.astype(F32)).astype(BF16)
    x1, h2, idx_t, gates_t = _post(
        proj, o_attn, x2, conv_w, w_branch_conv.astype(BF16), w_branch_attn.astype(BF16),
        w_out.astype(BF16), norm2_g.reshape(1, D), wr_hi, wr_lo, b_router.reshape(N_EXPERTS, 1),
        tm=512, seq_len=S)

    bm = MOE_BM
    nblk = T * TOP_K // bm + N_EXPERTS
    nblk_pad = -(-nblk // LANES) * LANES
    dest, be, nused = _route(idx_t, bm, nblk_pad)
    tt = ROW_TILE
    dest3 = dest.reshape(TOP_K, T // tt, tt).transpose(1, 0, 2)

    xs = _dispatch(dest3, h2, jnp.zeros((nblk * bm, D), F32))
    ys = _experts(be.reshape(nblk_pad), nused.reshape(LANES)[:1], xs,
                  w_gate_up.astype(BF16), b_gate_up.reshape(N_EXPERTS, 1, -1),
                  w_down.astype(BF16), b_down.reshape(N_EXPERTS, 1, -1), bm)
    return _combine(dest3, x1, gates_t.T, ys)


def kernel(x, norm1_g, w_in, q_norm_g, k_norm_g, rel_bias, conv_w, w_branch_conv, w_branch_attn, w_out, norm2_g, w_router, b_router, w_gate_up, b_gate_up, w_down, b_down):
    B, S, D = x.shape
    x2 = x.reshape(B * S, D)
    for l in range(norm1_g.shape[0]):
        x2 = _layer(x2, B, S, norm1_g[l], w_in[l], q_norm_g[l], k_norm_g[l], rel_bias, conv_w[l],
                    w_branch_conv[l], w_branch_attn[l], w_out[l], norm2_g[l], w_router[l],
                    b_router[l], w_gate_up[l], b_gate_up[l], w_down[l], b_down[l])
    return x2.reshape(B, S, D)
```

```python
import functools
import math

import numpy as np
import jax
import jax.numpy as jnp
from jax import lax
from jax.experimental import pallas as pl
from jax.experimental.pallas import tpu as pltpu

F32 = jnp.float32
BF16 = jnp.bfloat16
I32 = jnp.int32

HEAD_DIM = 64
ATTN_GROUPS = ((128, 1), (512, 4), (2048, 16))
HEADS_PER_GROUP = 4
N_GROUPS = len(ATTN_GROUPS)
ATTN_BLK = 128
N_BUCKETS = 32
MAX_DISTANCE = 2048
CONV_K = 3
N_EXPERTS = 32
TOP_K = 4
SWIGLU_LIMIT = 7.0
SWIGLU_ALPHA = 1.702
EPS = 1e-6
MASK_VALUE = -1e30

LANES = 128
VMEM_LIMIT = 52 * 1024 * 1024

MOE_BM = 256
ROW_TILE = 256


def _rms(x, gain):
    return x * lax.rsqrt(jnp.mean(x * x, axis=-1, keepdims=True) + EPS) * gain


def _dot(a, b):
    return jnp.dot(a, b, preferred_element_type=F32)


def _dot_nt(a, b):
    return lax.dot_general(a, b, (((1,), (1,)), ((), ())), preferred_element_type=F32)


def _inproj_kernel(x_ref, g_ref, w_ref, o_ref):
    h = _rms(x_ref[...], g_ref[...]).astype(BF16)
    o_ref[...] = _dot(h, w_ref[...]).astype(BF16)


def _inproj(x2, g1, w_in_bf, tm, tn):
    T, D = x2.shape
    N = w_in_bf.shape[1]
    return pl.pallas_call(
        _inproj_kernel,
        grid=(N // tn, T // tm),
        in_specs=[
            pl.BlockSpec((tm, D), lambda j, i: (i, 0)),
            pl.BlockSpec((1, D), lambda j, i: (0, 0)),
            pl.BlockSpec((D, tn), lambda j, i: (0, j)),
        ],
        out_specs=pl.BlockSpec((tm, tn), lambda j, i: (i, j)),
        out_shape=jax.ShapeDtypeStruct((T, N), BF16),
        compiler_params=pltpu.CompilerParams(
            dimension_semantics=("arbitrary", "arbitrary"), vmem_limit_bytes=VMEM_LIMIT),
        name="inproj",
    )(x2, g1, w_in_bf)


def _t5_bucket_np(dist):
    max_exact = N_BUCKETS // 2
    d = np.maximum(dist.astype(np.float64), 1.0)
    large = max_exact + (np.log(d / max_exact) / math.log(MAX_DISTANCE / max_exact)
                         * (N_BUCKETS - max_exact)).astype(np.int32)
    large = np.minimum(large, N_BUCKETS - 1)
    return np.where(dist < max_exact, dist, large)


def _attn_bias_tables(rel_bias):
    blk = ATTN_BLK
    qi = np.arange(blk)[:, None]
    kj = np.arange(2 * blk)[None, :]
    delta = qi + blk - kj
    band = (delta >= 0) & (delta <= blk)
    first = band & (kj >= blk)
    tables = []
    for gi, (_, dil) in enumerate(ATTN_GROUPS):
        bucket = _t5_bucket_np(np.clip(delta, 0, blk) * dil)
        hs = slice(gi * HEADS_PER_GROUP, (gi + 1) * HEADS_PER_GROUP)
        onehot = (jnp.asarray(bucket, I32)[..., None] == jnp.arange(N_BUCKETS, dtype=I32)).astype(F32)
        bias = jnp.einsum('qkb,bh->hqk', onehot, rel_bias[:, hs].astype(F32),
                          precision=lax.Precision.HIGHEST)
        rest = jnp.where(band[None], bias, MASK_VALUE)
        frst = jnp.where(first[None], bias, MASK_VALUE)
        tables.append(jnp.stack([rest, frst], axis=1))
    return jnp.stack(tables, axis=0)


def _attn_kernel(q0, q1, q2, k0, k1, k2, v0, v1, v2, gq_ref, gk_ref, bias_ref, o_ref,
                 qa, qb, ks, va, vb, acc_s, m_s, l_s):
    S = q0.shape[0]
    pad = ks.shape[0] - S
    blk = ATTN_BLK
    hd = HEAD_DIM
    q_refs, k_refs, v_refs = (q0, q1, q2), (k0, k1, k2), (v0, v1, v2)
    head_a = lax.broadcasted_iota(I32, (1, LANES), 1) < hd
    same_head = (lax.broadcasted_iota(I32, (LANES, LANES), 0) // hd
                 == lax.broadcasted_iota(I32, (LANES, LANES), 1) // hd).astype(BF16)

    zeros = jnp.zeros((pad, LANES), F32)
    ks[0:pad, :] = zeros
    va[0:pad, :] = zeros
    vb[0:pad, :] = zeros

    def head_norm(x, gain):
        sq = x * x
        hi = sq.astype(BF16)
        lo = (sq - hi.astype(F32)).astype(BF16)
        ss = _dot(hi, same_head) + _dot(lo, same_head)
        return x * lax.rsqrt(ss * (1.0 / hd) + EPS) * gain

    for g, (window, dil) in enumerate(ATTN_GROUPS):
        span = blk * dil
        nblk = S // span
        stride = dil if dil > 1 else None
        qn = head_norm(q_refs[g][...].astype(F32), gq_ref[...]) * (hd ** -0.5)
        qa[...] = jnp.where(head_a, qn, 0.0)
        qb[...] = jnp.where(head_a, 0.0, qn)
        ks[pad:pad + S, :] = head_norm(k_refs[g][...].astype(F32), gk_ref[...])
        v = v_refs[g][...].astype(F32)
        va[pad:pad + S, :] = jnp.where(head_a, v, 1.0)
        vb[pad:pad + S, :] = jnp.where(head_a, 1.0, v)

        def body(idx, carry, g=g, span=span, nblk=nblk, stride=stride):
            r = idx // nblk
            n = idx % nblk
            qstart = r + n * span
            rows = pl.ds(qstart, blk, stride=stride)
            win = pl.ds(pad + qstart - span, 2 * blk, stride=stride)
            first = jnp.where(n == 0, 1, 0)
            kw = ks[win, :].astype(BF16)
            res, ms = [], []
            for hh, (q_ref, v_ref) in enumerate(((qa, va), (qb, vb))):
                s = _dot_nt(q_ref[rows, :].astype(BF16), kw) + bias_ref[g, hh, first]
                m = jnp.max(s, axis=-1, keepdims=True)
                p = jnp.exp(s - m).astype(BF16)
                res.append(_dot(p, v_ref[win, :].astype(BF16)))
                ms.append(m)
            acc_s[g, rows, :] = jnp.where(head_a, res[0], res[1])
            l_s[g, rows, :] = pltpu.roll(jnp.where(head_a, res[1], res[0]), hd, axis=1)
            m_s[g, rows, :] = jnp.where(head_a, ms[0], ms[1])
            return carry

        lax.fori_loop(0, S // blk, body, 0, unroll=8)

    m_all = jnp.maximum(jnp.maximum(m_s[0], m_s[1]), m_s[2])
    num = jnp.zeros((S, LANES), F32)
    den = jnp.zeros((S, LANES), F32)
    for g in range(N_GROUPS):
        w = jnp.exp(m_s[g] - m_all)
        num = num + w * acc_s[g]
        den = den + w * l_s[g]
    o_ref[...] = (num / den).astype(BF16)


def _attention(proj, gq2, gk2, bias_tab, B, S, q_col, k_col, v_col):
    T = proj.shape[0]
    n_pairs = HEADS_PER_GROUP // 2

    def col_spec(base, g):
        return pl.BlockSpec((S, LANES), lambda b, p, base=base, g=g: (b, base + n_pairs * g + p))

    in_specs = ([col_spec(q_col, g) for g in range(N_GROUPS)]
                + [col_spec(k_col, g) for g in range(N_GROUPS)]
                + [col_spec(v_col, g) for g in range(N_GROUPS)]
                + [pl.BlockSpec((1, LANES), lambda b, p: (0, 0)),
                   pl.BlockSpec((1, LANES), lambda b, p: (0, 0)),
                   pl.BlockSpec((N_GROUPS, 2, 2, ATTN_BLK, 2 * ATTN_BLK), lambda b, p: (0, p, 0, 0, 0))])
    pad = ATTN_BLK * max(d for _, d in ATTN_GROUPS)
    return pl.pallas_call(
        _attn_kernel,
        grid=(B, n_pairs),
        in_specs=in_specs,
        out_specs=pl.BlockSpec((S, LANES), lambda b, p: (b, p)),
        out_shape=jax.ShapeDtypeStruct((T, HEADS_PER_GROUP * HEAD_DIM), BF16),
        scratch_shapes=[
            pltpu.VMEM((S, LANES), F32),
            pltpu.VMEM((S, LANES), F32),
            pltpu.VMEM((pad + S, LANES), F32),
            pltpu.VMEM((pad + S, LANES), F32),
            pltpu.VMEM((pad + S, LANES), F32),
            pltpu.VMEM((N_GROUPS, S, LANES), F32),
            pltpu.VMEM((N_GROUPS, S, LANES), F32),
            pltpu.VMEM((N_GROUPS, S, LANES), F32),
        ],
        compiler_params=pltpu.CompilerParams(
            dimension_semantics=("arbitrary", "arbitrary"), vmem_limit_bytes=VMEM_LIMIT),
        name="attn",
    )(*([proj] * 9), gq2, gk2, bias_tab)


def _post_kernel(cb_ref, cc_ref, cx_ref, gc_ref, ga_ref, ccp_ref, cxp_ref, o_ref, x_ref, cw_ref,
                 wbc_ref, wba_ref, wout_ref, g2_ref, wrh_ref, wrl_ref, br_ref,
                 x1_ref, h2_ref, idx_ref, gate_ref, *, seq_len):
    i = pl.program_id(0)
    tm = x_ref.shape[0]
    u = cc_ref[...].astype(F32) * cx_ref[...].astype(F32)
    prev = ccp_ref[...].astype(F32) * cxp_ref[...].astype(F32)
    prev = jnp.where((i * tm) % seq_len == 0, 0.0, prev)
    last = prev.shape[0] - 1
    p1 = prev[last:last + 1, :]
    p2 = prev[last - 1:last, :]
    rows = lax.broadcasted_iota(I32, (tm, 1), 0)
    u1 = jnp.where(rows == 0, p1, pltpu.roll(u, 1, axis=0))
    u2 = jnp.where(rows == 0, p2, jnp.where(rows == 1, p1, pltpu.roll(u, 2, axis=0)))
    conv = cw_ref[0:1, :] * u2 + cw_ref[1:2, :] * u1 + cw_ref[2:3, :] * u
    y_conv = _dot((cb_ref[...].astype(F32) * conv).astype(BF16), wbc_ref[...])
    y_attn = _dot(o_ref[...], wba_ref[...])
    merged = (jax.nn.sigmoid(gc_ref[...].astype(F32)) * y_conv
              + jax.nn.sigmoid(ga_ref[...].astype(F32)) * y_attn)
    x1 = x_ref[...] + _dot(merged.astype(BF16), wout_ref[...])
    x1_ref[...] = x1
    h2 = _rms(x1, g2_ref[...])
    h2_ref[...] = h2

    h_hi = h2.astype(BF16)
    h_lo = (h2 - h_hi.astype(F32)).astype(BF16)
    logits = (_dot_nt(wrh_ref[...], h_hi) + _dot_nt(wrh_ref[...], h_lo)
              + _dot_nt(wrl_ref[...], h_hi)) + br_ref[...]
    e_iota = lax.broadcasted_iota(I32, logits.shape, 0)
    work = logits
    vals, idxs = [], []
    for _ in range(TOP_K):
        mx = jnp.max(work, axis=0, keepdims=True)
        ix = jnp.min(jnp.where(work == mx, e_iota, N_EXPERTS), axis=0, keepdims=True)
        vals.append(mx)
        idxs.append(ix)
        work = jnp.where(e_iota == ix, -jnp.inf, work)
    ex = [jnp.exp(v - vals[0]) for v in vals]
    tot = ex[0] + ex[1] + ex[2] + ex[3]
    idx_ref[...] = jnp.concatenate(idxs, axis=0)
    gate_ref[...] = jnp.concatenate([e / tot for e in ex], axis=0)


def _post(proj, o_attn, x2, conv_w, wbc, wba, wout, g2, wr_hi, wr_lo, br, tm, seq_len):
    T, D = x2.shape
    prev_rows = 16
    kern = functools.partial(_post_kernel, seq_len=seq_len)

    def colblk(c):
        return pl.BlockSpec((tm, D), lambda i, c=c: (i, c))

    def prevblk(c):
        return pl.BlockSpec((prev_rows, D),
                            lambda i, c=c: (jnp.maximum(i * (tm // prev_rows) - 1, 0), c))

    def whole(a):
        return pl.BlockSpec(a.shape, lambda i, nd=a.ndim: (0,) * nd)

    return pl.pallas_call(
        kern,
        grid=(T // tm,),
        in_specs=[colblk(0), colblk(1), colblk(2), colblk(3), colblk(4), prevblk(1), prevblk(2),
                  pl.BlockSpec((tm, o_attn.shape[1]), lambda i: (i, 0)),
                  pl.BlockSpec((tm, D), lambda i: (i, 0)),
                  whole(conv_w), whole(wbc), whole(wba), whole(wout), whole(g2),
                  whole(wr_hi), whole(wr_lo), whole(br)],
        out_specs=[pl.BlockSpec((tm, D), lambda i: (i, 0)),
                   pl.BlockSpec((tm, D), lambda i: (i, 0)),
                   pl.BlockSpec((TOP_K, tm), lambda i: (0, i)),
                   pl.BlockSpec((TOP_K, tm), lambda i: (0, i))],
        out_shape=[jax.ShapeDtypeStruct((T, D), F32),
                   jax.ShapeDtypeStruct((T, D), F32),
                   jax.ShapeDtypeStruct((TOP_K, T), I32),
                   jax.ShapeDtypeStruct((TOP_K, T), F32)],
        compiler_params=pltpu.CompilerParams(
            dimension_semantics=("arbitrary",), vmem_limit_bytes=VMEM_LIMIT),
        name="post",
    )(proj, proj, proj, proj, proj, proj, proj, o_attn, x2, conv_w, wbc, wba, wout, g2,
      wr_hi, wr_lo, br)


def _route_kernel(idx_ref, dest_ref, be_ref, nused_ref, rank_s, *, bm, chunk):
    T = idx_ref.shape[1]
    e_iota = lax.broadcasted_iota(I32, (N_EXPERTS, chunk), 0)
    upper = (lax.broadcasted_iota(I32, (chunk, chunk), 0)
             < lax.broadcasted_iota(I32, (chunk, chunk), 1)).astype(BF16)

    def onehots(c):
        off = pl.multiple_of(c * chunk, chunk)
        idc = idx_ref[:, pl.ds(off, chunk)]
        return off, [e_iota == idc[k:k + 1, :] for k in range(TOP_K)]

    def pass1(c, carry):
        off, oh = onehots(c)
        member = oh[0].astype(F32) + oh[1].astype(F32) + oh[2].astype(F32) + oh[3].astype(F32)
        before = _dot(member.astype(BF16), upper) + carry
        ranks = [jnp.sum(jnp.where(o, before, 0.0), axis=0, keepdims=True) for o in oh]
        rank_s[:, pl.ds(off, chunk)] = jnp.concatenate(ranks, axis=0)
        return carry + jnp.sum(member, axis=1, keepdims=True)

    counts = lax.fori_loop(0, T // chunk, pass1, jnp.zeros((N_EXPERTS, 1), F32))

    nb = jnp.floor((counts + (bm - 1)) * (1.0 / bm))
    lower = (lax.broadcasted_iota(I32, (N_EXPERTS, N_EXPERTS), 1)
             < lax.broadcasted_iota(I32, (N_EXPERTS, N_EXPERTS), 0)).astype(BF16)
    boff = _dot(lower, jnp.broadcast_to(nb, (N_EXPERTS, LANES)).astype(BF16))[:, 0:1]
    poff = boff * bm
    bend = boff + nb

    def pass2(c, carry):
        off, oh = onehots(c)
        offs = [jnp.sum(jnp.where(o, poff, 0.0), axis=0, keepdims=True) for o in oh]
        dest = jnp.concatenate(offs, axis=0) + rank_s[:, pl.ds(off, chunk)]
        dest_ref[:, pl.ds(off, chunk)] = dest.astype(I32)
        return carry

    lax.fori_loop(0, T // chunk, pass2, 0)

    nbp = be_ref.shape[1]
    b_iota = lax.broadcasted_iota(I32, (N_EXPERTS, nbp), 1).astype(F32)
    be = jnp.sum((bend <= b_iota).astype(F32), axis=0, keepdims=True)
    be_ref[...] = jnp.minimum(be, N_EXPERTS - 1).astype(I32)
    nused_ref[...] = jnp.broadcast_to(bend[N_EXPERTS - 1:N_EXPERTS, :], nused_ref.shape).astype(I32)


def _route(idx_t, bm, nblk_pad):
    K, T = idx_t.shape
    kern = functools.partial(_route_kernel, bm=bm, chunk=512)
    return pl.pallas_call(
        kern,
        out_shape=[jax.ShapeDtypeStruct((K, T), I32),
                   jax.ShapeDtypeStruct((1, nblk_pad), I32),
                   jax.ShapeDtypeStruct((1, LANES), I32)],
        scratch_shapes=[pltpu.VMEM((K, T), F32)],
        compiler_params=pltpu.CompilerParams(vmem_limit_bytes=VMEM_LIMIT),
        name="route",
    )(idx_t)


def _row_copy(src_ref, src_row, dst_ref, dst_row, sem):
    return pltpu.make_async_copy(src_ref.at[pl.ds(src_row, 1), :], dst_ref.at[pl.ds(dst_row, 1), :], sem)


def _dispatch_kernel(dest_ref, h_ref, xs_in_ref, xs_ref, sem):
    del xs_in_ref
    tt = h_ref.shape[0]

    def issue(t, carry):
        for k in range(TOP_K):
            _row_copy(h_ref, t, xs_ref, dest_ref[k, t], sem).start(priority=k % 2)
        return carry

    lax.fori_loop(0, tt, issue, 0)

    def drain(t, carry):
        for k in range(TOP_K):
            _row_copy(h_ref, t, xs_ref, dest_ref[k, t], sem).wait()
        return carry

    lax.fori_loop(0, tt, drain, 0)


def _dispatch(dest3, h2, xs_zero):
    nt, K, tt = dest3.shape
    T, D = h2.shape
    return pl.pallas_call(
        _dispatch_kernel,
        grid=(nt,),
        in_specs=[pl.BlockSpec((None, K, tt), lambda i: (i, 0, 0), memory_space=pltpu.SMEM),
                  pl.BlockSpec((tt, D), lambda i: (i, 0)),
                  pl.BlockSpec(memory_space=pl.ANY)],
        out_specs=pl.BlockSpec(memory_space=pl.ANY),
        out_shape=jax.ShapeDtypeStruct(xs_zero.shape, xs_zero.dtype),
        scratch_shapes=[pltpu.SemaphoreType.DMA(())],
        input_output_aliases={2: 0},
        compiler_params=pltpu.CompilerParams(
            dimension_semantics=("arbitrary",), vmem_limit_bytes=VMEM_LIMIT),
        name="dispatch",
    )(dest3, h2, xs_zero)


def _expert_kernel(be_ref, nu_ref, xs_ref, wgu_ref, bgu_ref, wd_ref, bd_ref, ys_ref,
                   wgu_s, wd_s, *, chunk):
    b = pl.program_id(0)
    d_exp = wd_ref.shape[0]
    active = b < nu_ref[0]
    new_expert = jnp.logical_or(b == 0, be_ref[b] != be_ref[jnp.maximum(b - 1, 0)])

    @pl.when(jnp.logical_and(active, new_expert))
    def _():
        wgu_s[...] = wgu_ref[...].astype(BF16)
        wd_s[...] = wd_ref[...].astype(BF16)

    @pl.when(active)
    def _():
        x = xs_ref[...].astype(BF16)
        y = jnp.zeros(ys_ref.shape, F32)
        for c in range(d_exp // chunk):
            lo = c * chunk
            gate = _dot(x, wgu_s[:, lo:lo + chunk]) + bgu_ref[:, lo:lo + chunk]
            up = _dot(x, wgu_s[:, d_exp + lo:d_exp + lo + chunk]) + bgu_ref[:, d_exp + lo:d_exp + lo + chunk]
            gate = jnp.minimum(gate, SWIGLU_LIMIT)
            up = jnp.clip(up, -SWIGLU_LIMIT, SWIGLU_LIMIT)
            act = (up + 1.0) * (gate * jax.nn.sigmoid(SWIGLU_ALPHA * gate))
            y = y + _dot(act.astype(BF16), wd_s[lo:lo + chunk, :])
        ys_ref[...] = y + bd_ref[...]

    @pl.when(jnp.logical_not(active))
    def _():
        ys_ref[...] = jnp.zeros(ys_ref.shape, F32)


def _experts(be, nused, xs, wgu, bgu, wd, bd, bm):
    P, D = xs.shape
    E, _, two_de = wgu.shape
    d_exp = wd.shape[1]
    kern = functools.partial(_expert_kernel, chunk=256)

    def rows(b, be_r, nu_r):
        return (jnp.minimum(b, nu_r[0] - 1), 0)

    def per_expert(b, be_r, nu_r):
        return (be_r[b], 0, 0)

    grid_spec = pltpu.PrefetchScalarGridSpec(
        num_scalar_prefetch=2,
        grid=(P // bm,),
        in_specs=[pl.BlockSpec((bm, D), rows),
                  pl.BlockSpec((None, D, two_de), per_expert),
                  pl.BlockSpec((None, 1, two_de), per_expert),
                  pl.BlockSpec((None, d_exp, D), per_expert),
                  pl.BlockSpec((None, 1, D), per_expert)],
        out_specs=pl.BlockSpec((bm, D), lambda b, be_r, nu_r: (b, 0)),
        scratch_shapes=[pltpu.VMEM((D, two_de), BF16), pltpu.VMEM((d_exp, D), BF16)],
    )
    return pl.pallas_call(
        kern,
        grid_spec=grid_spec,
        out_shape=jax.ShapeDtypeStruct((P, D), F32),
        compiler_params=pltpu.CompilerParams(
            dimension_semantics=("arbitrary",), vmem_limit_bytes=VMEM_LIMIT),
        name="experts",
    )(be, nused, xs, wgu, bgu, wd, bd)


def _combine_kernel(dest_ref, x1_ref, gates_ref, ys_ref, out_ref, buf, sem):
    tt = x1_ref.shape[0]

    def issue(t, carry):
        for k in range(TOP_K):
            _row_copy(ys_ref, dest_ref[k, t], buf.at[k], t, sem).start(priority=k % 2)
        return carry

    lax.fori_loop(0, tt, issue, 0)

    def drain(t, carry):
        for k in range(TOP_K):
            _row_copy(ys_ref, dest_ref[k, t], buf.at[k], t, sem).wait()
        return carry

    lax.fori_loop(0, tt, drain, 0)

    acc = x1_ref[...]
    for k in range(TOP_K):
        acc = acc + gates_ref[:, k:k + 1] * buf[k]
    out_ref[...] = acc


def _combine(dest3, x1, gates, ys):
    nt, K, tt = dest3.shape
    T, D = x1.shape
    return pl.pallas_call(
        _combine_kernel,
        grid=(nt,),
        in_specs=[pl.BlockSpec((None, K, tt), lambda i: (i, 0, 0), memory_space=pltpu.SMEM),
                  pl.BlockSpec((tt, D), lambda i: (i, 0)),
                  pl.BlockSpec((tt, K), lambda i: (i, 0)),
                  pl.BlockSpec(memory_space=pl.ANY)],
        out_specs=pl.BlockSpec((tt, D), lambda i: (i, 0)),
        out_shape=jax.ShapeDtypeStruct((T, D), F32),
        scratch_shapes=[pltpu.VMEM((K, tt, D), F32), pltpu.SemaphoreType.DMA(())],
        compiler_params=pltpu.CompilerParams(
            dimension_semantics=("arbitrary",), vmem_limit_bytes=VMEM_LIMIT),
        name="combine",
    )(dest3, x1, gates, ys)


def _layer(x2, B, S, norm1_g, w_in, q_norm_g, k_norm_g, rel_bias, conv_w, w_branch_conv,
           w_branch_attn, w_out, norm2_g, w_router, b_router, w_gate_up, b_gate_up, w_down, b_down):
    T, D = x2.shape
    attn_w = N_GROUPS * HEADS_PER_GROUP * HEAD_DIM
    cuts = np.cumsum([0, D, D, D, attn_w, attn_w, attn_w, D, D])
    order = [0, 1, 2, 6, 7, 3, 4, 5]
    w_in_r = jnp.concatenate([w_in[:, cuts[s]:cuts[s + 1]] for s in order], axis=1).astype(BF16)
    q_col = 5 * D // LANES
    k_col = q_col + attn_w // LANES
    v_col = k_col + attn_w // LANES

    proj = _inproj(x2, norm1_g.reshape(1, D), w_in_r, tm=512, tn=w_in_r.shape[1] // 2)

    gq2 = jnp.tile(q_norm_g.reshape(1, HEAD_DIM), (1, LANES // HEAD_DIM))
    gk2 = jnp.tile(k_norm_g.reshape(1, HEAD_DIM), (1, LANES // HEAD_DIM))
    o_attn = _attention(proj, gq2, gk2, _attn_bias_tables(rel_bias), B, S, q_col, k_col, v_col)

    wr_hi = w_router.T.astype(BF16)
    wr_lo = (w_router.T - wr_hi.astype(F32)).astype(BF16)
    x1, h2, idx_t, gates_t = _post(
        proj, o_attn, x2, conv_w, w_branch_conv.astype(BF16), w_branch_attn.astype(BF16),
        w_out.astype(BF16), norm2_g.reshape(1, D), wr_hi, wr_lo, b_router.reshape(N_EXPERTS, 1),
        tm=512, seq_len=S)

    bm = MOE_BM
    nblk = T * TOP_K // bm + N_EXPERTS
    nblk_pad = -(-nblk // LANES) * LANES
    dest, be, nused = _route(idx_t, bm, nblk_pad)
    tt = ROW_TILE
    dest3 = dest.reshape(TOP_K, T // tt, tt).transpose(1, 0, 2)

    xs = _dispatch(dest3, h2, jnp.zeros((nblk * bm, D), F32))
    ys = _experts(be.reshape(nblk_pad), nused.reshape(LANES)[:1], xs,
                  w_gate_up, b_gate_up.reshape(N_EXPERTS, 1, -1),
                  w_down, b_down.reshape(N_EXPERTS, 1, -1), bm)
    return _combine(dest3, x1, gates_t.T, ys)


def kernel(x, norm1_g, w_in, q_norm_g, k_norm_g, rel_bias, conv_w, w_branch_conv, w_branch_attn, w_out, norm2_g, w_router, b_router, w_gate_up, b_gate_up, w_down, b_down):
    B, S, D = x.shape
    x2 = x.reshape(B * S, D)
    for l in range(norm1_g.shape[0]):
        x2 = _layer(x2, B, S, norm1_g[l], w_in[l], q_norm_g[l], k_norm_g[l], rel_bias, conv_w[l],
                    w_branch_conv[l], w_branch_attn[l], w_out[l], norm2_g[l], w_router[l],
                    b_router[l], w_gate_up[l], b_gate_up[l], w_down[l], b_down[l])
    return x2.reshape(B, S, D)
```

```python
import functools
import math

import numpy as np
import jax
import jax.numpy as jnp
from jax import lax
from jax.experimental import pallas as pl
from jax.experimental.pallas import tpu as pltpu

F32 = jnp.float32
BF16 = jnp.bfloat16
I32 = jnp.int32

HEAD_DIM = 64
ATTN_GROUPS = ((128, 1), (512, 4), (2048, 16))
HEADS_PER_GROUP = 4
N_GROUPS = len(ATTN_GROUPS)
ATTN_BLK = 128
N_BUCKETS = 32
MAX_DISTANCE = 2048
CONV_K = 3
N_EXPERTS = 32
TOP_K = 4
SWIGLU_LIMIT = 7.0
SWIGLU_ALPHA = 1.702
EPS = 1e-6
MASK_VALUE = -1e30

LANES = 128
VMEM_LIMIT = 52 * 1024 * 1024

MOE_BM = 512
ROW_TILE = 256


def _rms(x, gain):
    return x * lax.rsqrt(jnp.mean(x * x, axis=-1, keepdims=True) + EPS) * gain


def _dot(a, b):
    return jnp.dot(a, b, preferred_element_type=F32)


def _dot_nt(a, b):
    return lax.dot_general(a, b, (((1,), (1,)), ((), ())), preferred_element_type=F32)


def _inproj_kernel(x_ref, g_ref, w_ref, o_ref):
    h = _rms(x_ref[...], g_ref[...]).astype(BF16)
    o_ref[...] = _dot(h, w_ref[...]).astype(BF16)


def _inproj(x2, g1, w_in_bf, tm, tn):
    T, D = x2.shape
    N = w_in_bf.shape[1]
    return pl.pallas_call(
        _inproj_kernel,
        grid=(N // tn, T // tm),
        in_specs=[
            pl.BlockSpec((tm, D), lambda j, i: (i, 0)),
            pl.BlockSpec((1, D), lambda j, i: (0, 0)),
            pl.BlockSpec((D, tn), lambda j, i: (0, j)),
        ],
        out_specs=pl.BlockSpec((tm, tn), lambda j, i: (i, j)),
        out_shape=jax.ShapeDtypeStruct((T, N), BF16),
        compiler_params=pltpu.CompilerParams(
            dimension_semantics=("arbitrary", "arbitrary"), vmem_limit_bytes=VMEM_LIMIT),
        name="inproj",
    )(x2, g1, w_in_bf)


def _t5_bucket_np(dist):
    max_exact = N_BUCKETS // 2
    d = np.maximum(dist.astype(np.float64), 1.0)
    large = max_exact + (np.log(d / max_exact) / math.log(MAX_DISTANCE / max_exact)
                         * (N_BUCKETS - max_exact)).astype(np.int32)
    large = np.minimum(large, N_BUCKETS - 1)
    return np.where(dist < max_exact, dist, large)


def _attn_bias_tables(rel_bias):
    blk = ATTN_BLK
    qi = np.arange(blk)[:, None]
    kj = np.arange(2 * blk)[None, :]
    delta = qi + blk - kj
    band = (delta >= 0) & (delta <= blk)
    first = band & (kj >= blk)
    tables = []
    for gi, (_, dil) in enumerate(ATTN_GROUPS):
        bucket = _t5_bucket_np(np.clip(delta, 0, blk) * dil)
        hs = slice(gi * HEADS_PER_GROUP, (gi + 1) * HEADS_PER_GROUP)
        onehot = (jnp.asarray(bucket, I32)[..., None] == jnp.arange(N_BUCKETS, dtype=I32)).astype(F32)
        bias = jnp.einsum('qkb,bh->hqk', onehot, rel_bias[:, hs].astype(F32),
                          precision=lax.Precision.HIGHEST)
        rest = jnp.where(band[None], bias, MASK_VALUE)
        frst = jnp.where(first[None], bias, MASK_VALUE)
        tables.append(jnp.stack([rest, frst], axis=1))
    return jnp.stack(tables, axis=0)


def _attn_kernel(q0, q1, q2, k0, k1, k2, v0, v1, v2, gq_ref, gk_ref, bias_ref, o_ref,
                 qa, qb, ks, va, vb, acc_s, m_s, l_s):
    S = q0.shape[0]
    pad = ks.shape[0] - S
    blk = ATTN_BLK
    hd = HEAD_DIM
    q_refs, k_refs, v_refs = (q0, q1, q2), (k0, k1, k2), (v0, v1, v2)
    head_a = lax.broadcasted_iota(I32, (1, LANES), 1) < hd
    same_head = (lax.broadcasted_iota(I32, (LANES, LANES), 0) // hd
                 == lax.broadcasted_iota(I32, (LANES, LANES), 1) // hd).astype(BF16)

    zeros = jnp.zeros((pad, LANES), F32)
    ks[0:pad, :] = zeros
    va[0:pad, :] = zeros
    vb[0:pad, :] = zeros

    def head_norm(x, gain):
        sq = x * x
        hi = sq.astype(BF16)
        lo = (sq - hi.astype(F32)).astype(BF16)
        ss = _dot(hi, same_head) + _dot(lo, same_head)
        return x * lax.rsqrt(ss * (1.0 / hd) + EPS) * gain

    for g, (window, dil) in enumerate(ATTN_GROUPS):
        span = blk * dil
        nblk = S // span
        stride = dil if dil > 1 else None
        qn = head_norm(q_refs[g][...].astype(F32), gq_ref[...]) * (hd ** -0.5)
        qa[...] = jnp.where(head_a, qn, 0.0)
        qb[...] = jnp.where(head_a, 0.0, qn)
        ks[pad:pad + S, :] = head_norm(k_refs[g][...].astype(F32), gk_ref[...])
        v = v_refs[g][...].astype(F32)
        va[pad:pad + S, :] = jnp.where(head_a, v, 1.0)
        vb[pad:pad + S, :] = jnp.where(head_a, 1.0, v)

        def body(idx, carry, g=g, span=span, nblk=nblk, stride=stride):
            r = idx // nblk
            n = idx % nblk
            qstart = r + n * span
            rows = pl.ds(qstart, blk, stride=stride)
            win = pl.ds(pad + qstart - span, 2 * blk, stride=stride)
            first = jnp.where(n == 0, 1, 0)
            kw = ks[win, :].astype(BF16)
            res, ms = [], []
            for hh, (q_ref, v_ref) in enumerate(((qa, va), (qb, vb))):
                s = _dot_nt(q_ref[rows, :].astype(BF16), kw) + bias_ref[g, hh, first]
                m = jnp.max(s, axis=-1, keepdims=True)
                p = jnp.exp(s - m).astype(BF16)
                res.append(_dot(p, v_ref[win, :].astype(BF16)))
                ms.append(m)
            acc_s[g, rows, :] = jnp.where(head_a, res[0], res[1])
            l_s[g, rows, :] = pltpu.roll(jnp.where(head_a, res[1], res[0]), hd, axis=1)
            m_s[g, rows, :] = jnp.where(head_a, ms[0], ms[1])
            return carry

        lax.fori_loop(0, S // blk, body, 0, unroll=8)

    m_all = jnp.maximum(jnp.maximum(m_s[0], m_s[1]), m_s[2])
    num = jnp.zeros((S, LANES), F32)
    den = jnp.zeros((S, LANES), F32)
    for g in range(N_GROUPS):
        w = jnp.exp(m_s[g] - m_all)
        num = num + w * acc_s[g]
        den = den + w * l_s[g]
    o_ref[...] = (num / den).astype(BF16)


def _attention(proj, gq2, gk2, bias_tab, B, S, q_col, k_col, v_col):
    T = proj.shape[0]
    n_pairs = HEADS_PER_GROUP // 2

    def col_spec(base, g):
        return pl.BlockSpec((S, LANES), lambda b, p, base=base, g=g: (b, base + n_pairs * g + p))

    in_specs = ([col_spec(q_col, g) for g in range(N_GROUPS)]
                + [col_spec(k_col, g) for g in range(N_GROUPS)]
                + [col_spec(v_col, g) for g in range(N_GROUPS)]
                + [pl.BlockSpec((1, LANES), lambda b, p: (0, 0)),
                   pl.BlockSpec((1, LANES), lambda b, p: (0, 0)),
                   pl.BlockSpec((N_GROUPS, 2, 2, ATTN_BLK, 2 * ATTN_BLK), lambda b, p: (0, p, 0, 0, 0))])
    pad = ATTN_BLK * max(d for _, d in ATTN_GROUPS)
    return pl.pallas_call(
        _attn_kernel,
        grid=(B, n_pairs),
        in_specs=in_specs,
        out_specs=pl.BlockSpec((S, LANES), lambda b, p: (b, p)),
        out_shape=jax.ShapeDtypeStruct((T, HEADS_PER_GROUP * HEAD_DIM), BF16),
        scratch_shapes=[
            pltpu.VMEM((S, LANES), F32),
            pltpu.VMEM((S, LANES), F32),
            pltpu.VMEM((pad + S, LANES), F32),
            pltpu.VMEM((pad + S, LANES), F32),
            pltpu.VMEM((pad + S, LANES), F32),
            pltpu.VMEM((N_GROUPS, S, LANES), F32),
            pltpu.VMEM((N_GROUPS, S, LANES), F32),
            pltpu.VMEM((N_GROUPS, S, LANES), F32),
        ],
        compiler_params=pltpu.CompilerParams(
            dimension_semantics=("arbitrary", "arbitrary"), vmem_limit_bytes=VMEM_LIMIT),
        name="attn",
    )(*([proj] * 9), gq2, gk2, bias_tab)


def _post_kernel(cb_ref, cc_ref, cx_ref, gc_ref, ga_ref, ccp_ref, cxp_ref, o_ref, x_ref, cw_ref,
                 wbc_ref, wba_ref, wout_ref, g2_ref, wrh_ref, wrl_ref, br_ref,
                 x1_ref, h2_ref, idx_ref, gate_ref, *, seq_len):
    i = pl.program_id(0)
    tm = x_ref.shape[0]
    u = cc_ref[...].astype(F32) * cx_ref[...].astype(F32)
    prev = ccp_ref[...].astype(F32) * cxp_ref[...].astype(F32)
    prev = jnp.where((i * tm) % seq_len == 0, 0.0, prev)
    last = prev.shape[0] - 1
    p1 = prev[last:last + 1, :]
    p2 = prev[last - 1:last, :]
    rows = lax.broadcasted_iota(I32, (tm, 1), 0)
    u1 = jnp.where(rows == 0, p1, pltpu.roll(u, 1, axis=0))
    u2 = jnp.where(rows == 0, p2, jnp.where(rows == 1, p1, pltpu.roll(u, 2, axis=0)))
    conv = cw_ref[0:1, :] * u2 + cw_ref[1:2, :] * u1 + cw_ref[2:3, :] * u
    y_conv = _dot((cb_ref[...].astype(F32) * conv).astype(BF16), wbc_ref[...])
    y_attn = _dot(o_ref[...], wba_ref[...])
    merged = (jax.nn.sigmoid(gc_ref[...].astype(F32)) * y_conv
              + jax.nn.sigmoid(ga_ref[...].astype(F32)) * y_attn)
    x1 = x_ref[...] + _dot(merged.astype(BF16), wout_ref[...])
    x1_ref[...] = x1
    h2 = _rms(x1, g2_ref[...])
    h2_ref[...] = h2

    h_hi = h2.astype(BF16)
    h_lo = (h2 - h_hi.astype(F32)).astype(BF16)
    logits = (_dot_nt(wrh_ref[...], h_hi) + _dot_nt(wrh_ref[...], h_lo)
              + _dot_nt(wrl_ref[...], h_hi)) + br_ref[...]
    e_iota = lax.broadcasted_iota(I32, logits.shape, 0)
    work = logits
    vals, idxs = [], []
    for _ in range(TOP_K):
        mx = jnp.max(work, axis=0, keepdims=True)
        ix = jnp.min(jnp.where(work == mx, e_iota, N_EXPERTS), axis=0, keepdims=True)
        vals.append(mx)
        idxs.append(ix)
        work = jnp.where(e_iota == ix, -jnp.inf, work)
    ex = [jnp.exp(v - vals[0]) for v in vals]
    tot = ex[0] + ex[1] + ex[2] + ex[3]
    idx_ref[...] = jnp.concatenate(idxs, axis=0)
    gate_ref[...] = jnp.concatenate([e / tot for e in ex], axis=0)


def _post(proj, o_attn, x2, conv_w, wbc, wba, wout, g2, wr_hi, wr_lo, br, tm, seq_len):
    T, D = x2.shape
    prev_rows = 16
    kern = functools.partial(_post_kernel, seq_len=seq_len)

    def colblk(c):
        return pl.BlockSpec((tm, D), lambda i, c=c: (i, c))

    def prevblk(c):
        return pl.BlockSpec((prev_rows, D),
                            lambda i, c=c: (jnp.maximum(i * (tm // prev_rows) - 1, 0), c))

    def whole(a):
        return pl.BlockSpec(a.shape, lambda i, nd=a.ndim: (0,) * nd)

    return pl.pallas_call(
        kern,
        grid=(T // tm,),
        in_specs=[colblk(0), colblk(1), colblk(2), colblk(3), colblk(4), prevblk(1), prevblk(2),
                  pl.BlockSpec((tm, o_attn.shape[1]), lambda i: (i, 0)),
                  pl.BlockSpec((tm, D), lambda i: (i, 0)),
                  whole(conv_w), whole(wbc), whole(wba), whole(wout), whole(g2),
                  whole(wr_hi), whole(wr_lo), whole(br)],
        out_specs=[pl.BlockSpec((tm, D), lambda i: (i, 0)),
                   pl.BlockSpec((tm, D), lambda i: (i, 0)),
                   pl.BlockSpec((TOP_K, tm), lambda i: (0, i)),
                   pl.BlockSpec((TOP_K, tm), lambda i: (0, i))],
        out_shape=[jax.ShapeDtypeStruct((T, D), F32),
                   jax.ShapeDtypeStruct((T, D), F32),
                   jax.ShapeDtypeStruct((TOP_K, T), I32),
                   jax.ShapeDtypeStruct((TOP_K, T), F32)],
        compiler_params=pltpu.CompilerParams(
            dimension_semantics=("arbitrary",), vmem_limit_bytes=VMEM_LIMIT),
        name="post",
    )(proj, proj, proj, proj, proj, proj, proj, o_attn, x2, conv_w, wbc, wba, wout, g2,
      wr_hi, wr_lo, br)


def _route_kernel(idx_ref, dest_ref, be_ref, ends_ref, rank_s, *, bm, chunk):
    T = idx_ref.shape[1]
    e_iota = lax.broadcasted_iota(I32, (N_EXPERTS, chunk), 0)
    upper = (lax.broadcasted_iota(I32, (chunk, chunk), 0)
             < lax.broadcasted_iota(I32, (chunk, chunk), 1)).astype(BF16)

    def onehots(c):
        off = pl.multiple_of(c * chunk, chunk)
        idc = idx_ref[:, pl.ds(off, chunk)]
        return off, [e_iota == idc[k:k + 1, :] for k in range(TOP_K)]

    def pass1(c, carry):
        off, oh = onehots(c)
        member = oh[0].astype(F32) + oh[1].astype(F32) + oh[2].astype(F32) + oh[3].astype(F32)
        before = _dot(member.astype(BF16), upper) + carry
        ranks = [jnp.sum(jnp.where(o, before, 0.0), axis=0, keepdims=True) for o in oh]
        rank_s[:, pl.ds(off, chunk)] = jnp.concatenate(ranks, axis=0)
        return carry + jnp.sum(member, axis=1, keepdims=True)

    counts = lax.fori_loop(0, T // chunk, pass1, jnp.zeros((N_EXPERTS, 1), F32))

    nb = jnp.floor((counts + (bm - 1)) * (1.0 / bm))
    lower = (lax.broadcasted_iota(I32, (N_EXPERTS, N_EXPERTS), 1)
             < lax.broadcasted_iota(I32, (N_EXPERTS, N_EXPERTS), 0)).astype(BF16)
    boff = _dot(lower, jnp.broadcast_to(nb, (N_EXPERTS, LANES)).astype(BF16))[:, 0:1]
    poff = boff * bm
    bend = boff + nb

    def pass2(c, carry):
        off, oh = onehots(c)
        offs = [jnp.sum(jnp.where(o, poff, 0.0), axis=0, keepdims=True) for o in oh]
        dest = jnp.concatenate(offs, axis=0) + rank_s[:, pl.ds(off, chunk)]
        dest_ref[:, pl.ds(off, chunk)] = dest.astype(I32)
        return carry

    lax.fori_loop(0, T // chunk, pass2, 0)

    nbp = be_ref.shape[1]
    b_iota = lax.broadcasted_iota(I32, (N_EXPERTS, nbp), 1).astype(F32)
    be = jnp.sum((bend <= b_iota).astype(F32), axis=0, keepdims=True)
    be_ref[...] = jnp.minimum(be, N_EXPERTS - 1).astype(I32)
    diag = (lax.broadcasted_iota(I32, (N_EXPERTS, LANES), 0)
            == lax.broadcasted_iota(I32, (N_EXPERTS, LANES), 1))
    ends_ref[...] = jnp.sum(jnp.where(diag, bend * bm, 0.0), axis=0, keepdims=True).astype(I32)


def _route(idx_t, bm, nblk_pad):
    K, T = idx_t.shape
    kern = functools.partial(_route_kernel, bm=bm, chunk=512)
    return pl.pallas_call(
        kern,
        out_shape=[jax.ShapeDtypeStruct((K, T), I32),
                   jax.ShapeDtypeStruct((1, nblk_pad), I32),
                   jax.ShapeDtypeStruct((1, LANES), I32)],
        scratch_shapes=[pltpu.VMEM((K, T), F32)],
        compiler_params=pltpu.CompilerParams(vmem_limit_bytes=VMEM_LIMIT),
        name="route",
    )(idx_t)


def _row_copy(src_ref, src_row, dst_ref, dst_row, sem):
    return pltpu.make_async_copy(src_ref.at[pl.ds(src_row, 1), :], dst_ref.at[pl.ds(dst_row, 1), :], sem)


def _dispatch_kernel(ends_ref, dest_ref, h_ref, xs_ref, zbuf, sem, zsem):
    tt = h_ref.shape[0]
    bm = zbuf.shape[0]

    @pl.when(pl.program_id(0) == 0)
    def _():
        zbuf[...] = jnp.zeros(zbuf.shape, F32)
        fills = []
        for e in range(N_EXPERTS):
            end = ends_ref[e]
            start = ends_ref[e - 1] if e > 0 else 0
            fills.append((end > start,
                          pltpu.make_async_copy(
                              zbuf, xs_ref.at[pl.ds(pl.multiple_of(end - bm, bm), bm), :], zsem)))
        for nonempty, fill in fills:
            pl.when(nonempty)(fill.start)
        for nonempty, fill in fills:
            pl.when(nonempty)(fill.wait)

        def tail(b):
            return pltpu.make_async_copy(
                zbuf, xs_ref.at[pl.ds(pl.multiple_of(b * bm, bm), bm), :], zsem)

        first_unused = ends_ref[N_EXPERTS - 1] // bm
        n_blocks = xs_ref.shape[0] // bm
        lax.fori_loop(first_unused, n_blocks, lambda b, c: (tail(b).start(), c)[1], 0)
        lax.fori_loop(first_unused, n_blocks, lambda b, c: (tail(b).wait(), c)[1], 0)

    def issue(t, carry):
        for k in range(TOP_K):
            _row_copy(h_ref, t, xs_ref, dest_ref[k, t], sem).start(priority=k % 2)
        return carry

    lax.fori_loop(0, tt, issue, 0, unroll=8)

    def drain(t, carry):
        for k in range(TOP_K):
            _row_copy(h_ref, t, xs_ref, dest_ref[k, t], sem).wait()
        return carry

    lax.fori_loop(0, tt, drain, 0)


def _dispatch(ends, dest3, h2, n_rows, bm):
    nt, K, tt = dest3.shape
    T, D = h2.shape
    grid_spec = pltpu.PrefetchScalarGridSpec(
        num_scalar_prefetch=1,
        grid=(nt,),
        in_specs=[pl.BlockSpec((None, K, tt), lambda i, ends_r: (i, 0, 0), memory_space=pltpu.SMEM),
                  pl.BlockSpec((tt, D), lambda i, ends_r: (i, 0))],
        out_specs=pl.BlockSpec(memory_space=pl.ANY),
        scratch_shapes=[pltpu.VMEM((bm, D), F32), pltpu.SemaphoreType.DMA(()),
                        pltpu.SemaphoreType.DMA(())],
    )
    return pl.pallas_call(
        _dispatch_kernel,
        grid_spec=grid_spec,
        out_shape=jax.ShapeDtypeStruct((n_rows, D), F32),
        compiler_params=pltpu.CompilerParams(
            dimension_semantics=("arbitrary",), vmem_limit_bytes=VMEM_LIMIT),
        name="dispatch",
    )(ends, dest3, h2)


def _expert_kernel(be_ref, nu_ref, xs_ref, wgu_ref, bgu_ref, wd_ref, bd_ref, ys_ref,
                   wgu_s, wd_s, *, chunk):
    b = pl.program_id(0)
    d_exp = wd_ref.shape[0]
    active = b < nu_ref[0]
    new_expert = jnp.logical_or(b == 0, be_ref[b] != be_ref[jnp.maximum(b - 1, 0)])

    @pl.when(jnp.logical_and(active, new_expert))
    def _():
        wgu_s[...] = wgu_ref[...].astype(BF16)
        wd_s[...] = wd_ref[...].astype(BF16)

    @pl.when(active)
    def _():
        x = xs_ref[...].astype(BF16)
        y = jnp.zeros(ys_ref.shape, F32)
        for c in range(d_exp // chunk):
            lo = c * chunk
            gate = _dot(x, wgu_s[:, lo:lo + chunk]) + bgu_ref[:, lo:lo + chunk]
            up = _dot(x, wgu_s[:, d_exp + lo:d_exp + lo + chunk]) + bgu_ref[:, d_exp + lo:d_exp + lo + chunk]
            gate = jnp.minimum(gate, SWIGLU_LIMIT)
            up = jnp.clip(up, -SWIGLU_LIMIT, SWIGLU_LIMIT)
            act = (up + 1.0) * (gate * jax.nn.sigmoid(SWIGLU_ALPHA * gate))
            y = y + _dot(act.astype(BF16), wd_s[lo:lo + chunk, :])
        ys_ref[...] = y + bd_ref[...]

    @pl.when(jnp.logical_not(active))
    def _():
        ys_ref[...] = jnp.zeros(ys_ref.shape, F32)


def _experts(be, nused, xs, wgu, bgu, wd, bd, bm):
    P, D = xs.shape
    E, _, two_de = wgu.shape
    d_exp = wd.shape[1]
    kern = functools.partial(_expert_kernel, chunk=256)

    def rows(b, be_r, nu_r):
        return (jnp.minimum(b, nu_r[0] - 1), 0)

    def per_expert(b, be_r, nu_r):
        return (be_r[b], 0, 0)

    grid_spec = pltpu.PrefetchScalarGridSpec(
        num_scalar_prefetch=2,
        grid=(P // bm,),
        in_specs=[pl.BlockSpec((bm, D), rows),
                  pl.BlockSpec((None, D, two_de), per_expert),
                  pl.BlockSpec((None, 1, two_de), per_expert),
                  pl.BlockSpec((None, d_exp, D), per_expert),
                  pl.BlockSpec((None, 1, D), per_expert)],
        out_specs=pl.BlockSpec((bm, D), lambda b, be_r, nu_r: (b, 0)),
        scratch_shapes=[pltpu.VMEM((D, two_de), BF16), pltpu.VMEM((d_exp, D), BF16)],
    )
    return pl.pallas_call(
        kern,
        grid_spec=grid_spec,
        out_shape=jax.ShapeDtypeStruct((P, D), F32),
        compiler_params=pltpu.CompilerParams(
            dimension_semantics=("arbitrary",), vmem_limit_bytes=VMEM_LIMIT),
        name="experts",
    )(be, nused, xs, wgu, bgu, wd, bd)


def _combine_kernel(dest_ref, x1_ref, gates_ref, ys_ref, out_ref, buf, sem):
    tt = x1_ref.shape[0]

    def issue(t, carry):
        for k in range(TOP_K):
            _row_copy(ys_ref, dest_ref[k, t], buf.at[k], t, sem).start(priority=k % 2)
        return carry

    lax.fori_loop(0, tt, issue, 0, unroll=8)

    def drain(t, carry):
        for k in range(TOP_K):
            _row_copy(ys_ref, dest_ref[k, t], buf.at[k], t, sem).wait()
        return carry

    lax.fori_loop(0, tt, drain, 0)

    acc = x1_ref[...]
    for k in range(TOP_K):
        acc = acc + gates_ref[:, k:k + 1] * buf[k]
    out_ref[...] = acc


def _combine(dest3, x1, gates, ys):
    nt, K, tt = dest3.shape
    T, D = x1.shape
    return pl.pallas_call(
        _combine_kernel,
        grid=(nt,),
        in_specs=[pl.BlockSpec((None, K, tt), lambda i: (i, 0, 0), memory_space=pltpu.SMEM),
                  pl.BlockSpec((tt, D), lambda i: (i, 0)),
                  pl.BlockSpec((tt, K), lambda i: (i, 0)),
                  pl.BlockSpec(memory_space=pl.ANY)],
        out_specs=pl.BlockSpec((tt, D), lambda i: (i, 0)),
        out_shape=jax.ShapeDtypeStruct((T, D), F32),
        scratch_shapes=[pltpu.VMEM((K, tt, D), F32), pltpu.SemaphoreType.DMA(())],
        compiler_params=pltpu.CompilerParams(
            dimension_semantics=("arbitrary",), vmem_limit_bytes=VMEM_LIMIT),
        name="combine",
    )(dest3, x1, gates, ys)


def _layer(x2, B, S, norm1_g, w_in, q_norm_g, k_norm_g, rel_bias, conv_w, w_branch_conv,
           w_branch_attn, w_out, norm2_g, w_router, b_router, w_gate_up, b_gate_up, w_down, b_down):
    T, D = x2.shape
    attn_w = N_GROUPS * HEADS_PER_GROUP * HEAD_DIM
    cuts = np.cumsum([0, D, D, D, attn_w, attn_w, attn_w, D, D])
    order = [0, 1, 2, 6, 7, 3, 4, 5]
    w_in_r = jnp.concatenate([w_in[:, cuts[s]:cuts[s + 1]] for s in order], axis=1).astype(BF16)
    q_col = 5 * D // LANES
    k_col = q_col + attn_w // LANES
    v_col = k_col + attn_w // LANES

    proj = _inproj(x2, norm1_g.reshape(1, D), w_in_r, tm=512, tn=w_in_r.shape[1] // 2)

    gq2 = jnp.tile(q_norm_g.reshape(1, HEAD_DIM), (1, LANES // HEAD_DIM))
    gk2 = jnp.tile(k_norm_g.reshape(1, HEAD_DIM), (1, LANES // HEAD_DIM))
    o_attn = _attention(proj, gq2, gk2, _attn_bias_tables(rel_bias), B, S, q_col, k_col, v_col)

    wr_hi = w_router.T.astype(BF16)
    wr_lo = (w_router.T - wr_hi.astype(F32)).astype(BF16)
    x1, h2, idx_t, gates_t = _post(
        proj, o_attn, x2, conv_w, w_branch_conv.astype(BF16), w_branch_attn.astype(BF16),
        w_out.astype(BF16), norm2_g.reshape(1, D), wr_hi, wr_lo, b_router.reshape(N_EXPERTS, 1),
        tm=512, seq_len=S)

    bm = MOE_BM
    nblk = T * TOP_K // bm + N_EXPERTS
    nblk_pad = -(-nblk // LANES) * LANES
    dest, be, ends = _route(idx_t, bm, nblk_pad)
    ends = ends.reshape(LANES)
    nused = ends[N_EXPERTS - 1:N_EXPERTS] // bm
    tt = ROW_TILE
    dest3 = dest.reshape(TOP_K, T // tt, tt).transpose(1, 0, 2)

    xs = _dispatch(ends, dest3, h2, nblk * bm, bm)
    ys = _experts(be.reshape(nblk_pad), nused, xs,
                  w_gate_up, b_gate_up.reshape(N_EXPERTS, 1, -1),
                  w_down, b_down.reshape(N_EXPERTS, 1, -1), bm)
    return _combine(dest3, x1, gates_t.T, ys)


def kernel(x, norm1_g, w_in, q_norm_g, k_norm_g, rel_bias, conv_w, w_branch_conv, w_branch_attn, w_out, norm2_g, w_router, b_router, w_gate_up, b_gate_up, w_down, b_down):
    B, S, D = x.shape
    x2 = x.reshape(B * S, D)
    for l in range(norm1_g.shape[0]):
        x2 = _layer(x2, B, S, norm1_g[l], w_in[l], q_norm_g[l], k_norm_g[l], rel_bias, conv_w[l],
                    w_branch_conv[l], w_branch_attn[l], w_out[l], norm2_g[l], w_router[l],
                    b_router[l], w_gate_up[l], b_gate_up[l], w_down[l], b_down[l])
    return x2.reshape(B, S, D)
```

```python
import functools
import math

import numpy as np
import jax
import jax.numpy as jnp
from jax import lax
from jax.experimental import pallas as pl
from jax.experimental.pallas import tpu as pltpu

F32 = jnp.float32
BF16 = jnp.bfloat16
I32 = jnp.int32

HEAD_DIM = 64
ATTN_GROUPS = ((128, 1), (512, 4), (2048, 16))
HEADS_PER_GROUP = 4
N_GROUPS = len(ATTN_GROUPS)
ATTN_BLK = 128
N_BUCKETS = 32
MAX_DISTANCE = 2048
CONV_K = 3
N_EXPERTS = 32
TOP_K = 4
SWIGLU_LIMIT = 7.0
SWIGLU_ALPHA = 1.702
EPS = 1e-6
MASK_VALUE = -1e30

LANES = 128
VMEM_LIMIT = 52 * 1024 * 1024

MOE_BM = 512
ROW_TILE = 512
ISSUE_GROUP = 8


def _rms(x, gain):
    return x * lax.rsqrt(jnp.mean(x * x, axis=-1, keepdims=True) + EPS) * gain


def _dot(a, b):
    return jnp.dot(a, b, preferred_element_type=F32)


def _dot_nt(a, b):
    return lax.dot_general(a, b, (((1,), (1,)), ((), ())), preferred_element_type=F32)


def _inproj_kernel(x_ref, g_ref, w_ref, o_ref):
    h = _rms(x_ref[...], g_ref[...]).astype(BF16)
    o_ref[...] = _dot(h, w_ref[...]).astype(BF16)


def _inproj(x2, g1, w_in_bf, tm, tn):
    T, D = x2.shape
    N = w_in_bf.shape[1]
    return pl.pallas_call(
        _inproj_kernel,
        grid=(N // tn, T // tm),
        in_specs=[
            pl.BlockSpec((tm, D), lambda j, i: (i, 0)),
            pl.BlockSpec((1, D), lambda j, i: (0, 0)),
            pl.BlockSpec((D, tn), lambda j, i: (0, j)),
        ],
        out_specs=pl.BlockSpec((tm, tn), lambda j, i: (i, j)),
        out_shape=jax.ShapeDtypeStruct((T, N), BF16),
        compiler_params=pltpu.CompilerParams(
            dimension_semantics=("arbitrary", "arbitrary"), vmem_limit_bytes=VMEM_LIMIT),
        name="inproj",
    )(x2, g1, w_in_bf)


def _t5_bucket_np(dist):
    max_exact = N_BUCKETS // 2
    d = np.maximum(dist.astype(np.float64), 1.0)
    large = max_exact + (np.log(d / max_exact) / math.log(MAX_DISTANCE / max_exact)
                         * (N_BUCKETS - max_exact)).astype(np.int32)
    large = np.minimum(large, N_BUCKETS - 1)
    return np.where(dist < max_exact, dist, large)


def _attn_bias_tables(rel_bias):
    blk = ATTN_BLK
    qi = np.arange(blk)[:, None]
    kj = np.arange(2 * blk)[None, :]
    delta = qi + blk - kj
    band = (delta >= 0) & (delta <= blk)
    first = band & (kj >= blk)
    tables = []
    for gi, (_, dil) in enumerate(ATTN_GROUPS):
        bucket = _t5_bucket_np(np.clip(delta, 0, blk) * dil)
        hs = slice(gi * HEADS_PER_GROUP, (gi + 1) * HEADS_PER_GROUP)
        onehot = (jnp.asarray(bucket, I32)[..., None] == jnp.arange(N_BUCKETS, dtype=I32)).astype(F32)
        bias = jnp.einsum('qkb,bh->hqk', onehot, rel_bias[:, hs].astype(F32),
                          precision=lax.Precision.HIGHEST)
        rest = jnp.where(band[None], bias, MASK_VALUE)
        frst = jnp.where(first[None], bias, MASK_VALUE)
        tables.append(jnp.stack([rest, frst], axis=1))
    return jnp.stack(tables, axis=0)


def _attn_kernel(q0, q1, q2, k0, k1, k2, v0, v1, v2, gq_ref, gk_ref, bias_ref, o_ref,
                 qa, qb, ks, va, vb, acc_s, m_s, l_s):
    S = q0.shape[0]
    pad = ks.shape[0] - S
    blk = ATTN_BLK
    hd = HEAD_DIM
    q_refs, k_refs, v_refs = (q0, q1, q2), (k0, k1, k2), (v0, v1, v2)
    head_a = lax.broadcasted_iota(I32, (1, LANES), 1) < hd
    same_head = (lax.broadcasted_iota(I32, (LANES, LANES), 0) // hd
                 == lax.broadcasted_iota(I32, (LANES, LANES), 1) // hd).astype(BF16)

    zeros = jnp.zeros((pad, LANES), F32)
    ks[0:pad, :] = zeros
    va[0:pad, :] = zeros
    vb[0:pad, :] = zeros

    def head_norm(x, gain):
        sq = x * x
        hi = sq.astype(BF16)
        lo = (sq - hi.astype(F32)).astype(BF16)
        ss = _dot(hi, same_head) + _dot(lo, same_head)
        return x * lax.rsqrt(ss * (1.0 / hd) + EPS) * gain

    for g, (window, dil) in enumerate(ATTN_GROUPS):
        span = blk * dil
        nblk = S // span
        stride = dil if dil > 1 else None
        qn = head_norm(q_refs[g][...].astype(F32), gq_ref[...]) * (hd ** -0.5)
        qa[...] = jnp.where(head_a, qn, 0.0)
        qb[...] = jnp.where(head_a, 0.0, qn)
        ks[pad:pad + S, :] = head_norm(k_refs[g][...].astype(F32), gk_ref[...])
        v = v_refs[g][...].astype(F32)
        va[pad:pad + S, :] = jnp.where(head_a, v, 1.0)
        vb[pad:pad + S, :] = jnp.where(head_a, 1.0, v)

        def body(idx, carry, g=g, span=span, nblk=nblk, stride=stride):
            r = idx // nblk
            n = idx % nblk
            qstart = r + n * span
            rows = pl.ds(qstart, blk, stride=stride)
            win = pl.ds(pad + qstart - span, 2 * blk, stride=stride)
            first = jnp.where(n == 0, 1, 0)
            kw = ks[win, :].astype(BF16)
            res, ms = [], []
            for hh, (q_ref, v_ref) in enumerate(((qa, va), (qb, vb))):
                s = _dot_nt(q_ref[rows, :].astype(BF16), kw) + bias_ref[g, hh, first]
                m = jnp.max(s, axis=-1, keepdims=True)
                p = jnp.exp(s - m).astype(BF16)
                res.append(_dot(p, v_ref[win, :].astype(BF16)))
                ms.append(m)
            acc_s[g, rows, :] = jnp.where(head_a, res[0], res[1])
            l_s[g, rows, :] = pltpu.roll(jnp.where(head_a, res[1], res[0]), hd, axis=1)
            m_s[g, rows, :] = jnp.where(head_a, ms[0], ms[1])
            return carry

        lax.fori_loop(0, S // blk, body, 0, unroll=8)

    m_all = jnp.maximum(jnp.maximum(m_s[0], m_s[1]), m_s[2])
    num = jnp.zeros((S, LANES), F32)
    den = jnp.zeros((S, LANES), F32)
    for g in range(N_GROUPS):
        w = jnp.exp(m_s[g] - m_all)
        num = num + w * acc_s[g]
        den = den + w * l_s[g]
    o_ref[...] = (num / den).astype(BF16)


def _attention(proj, gq2, gk2, bias_tab, B, S, q_col, k_col, v_col):
    T = proj.shape[0]
    n_pairs = HEADS_PER_GROUP // 2

    def col_spec(base, g):
        return pl.BlockSpec((S, LANES), lambda b, p, base=base, g=g: (b, base + n_pairs * g + p))

    in_specs = ([col_spec(q_col, g) for g in range(N_GROUPS)]
                + [col_spec(k_col, g) for g in range(N_GROUPS)]
                + [col_spec(v_col, g) for g in range(N_GROUPS)]
                + [pl.BlockSpec((1, LANES), lambda b, p: (0, 0)),
                   pl.BlockSpec((1, LANES), lambda b, p: (0, 0)),
                   pl.BlockSpec((N_GROUPS, 2, 2, ATTN_BLK, 2 * ATTN_BLK), lambda b, p: (0, p, 0, 0, 0))])
    pad = ATTN_BLK * max(d for _, d in ATTN_GROUPS)
    return pl.pallas_call(
        _attn_kernel,
        grid=(B, n_pairs),
        in_specs=in_specs,
        out_specs=pl.BlockSpec((S, LANES), lambda b, p: (b, p)),
        out_shape=jax.ShapeDtypeStruct((T, HEADS_PER_GROUP * HEAD_DIM), BF16),
        scratch_shapes=[
            pltpu.VMEM((S, LANES), F32),
            pltpu.VMEM((S, LANES), F32),
            pltpu.VMEM((pad + S, LANES), F32),
            pltpu.VMEM((pad + S, LANES), F32),
            pltpu.VMEM((pad + S, LANES), F32),
            pltpu.VMEM((N_GROUPS, S, LANES), F32),
            pltpu.VMEM((N_GROUPS, S, LANES), F32),
            pltpu.VMEM((N_GROUPS, S, LANES), F32),
        ],
        compiler_params=pltpu.CompilerParams(
            dimension_semantics=("arbitrary", "arbitrary"), vmem_limit_bytes=VMEM_LIMIT),
        name="attn",
    )(*([proj] * 9), gq2, gk2, bias_tab)


def _post_kernel(cb_ref, cc_ref, cx_ref, gc_ref, ga_ref, ccp_ref, cxp_ref, o_ref, x_ref, cw_ref,
                 wbc_ref, wba_ref, wout_ref, g2_ref, wrh_ref, wrl_ref, br_ref,
                 x1_ref, h2_ref, idx_ref, gate_ref, *, seq_len):
    i = pl.program_id(0)
    tm = x_ref.shape[0]
    u = cc_ref[...].astype(F32) * cx_ref[...].astype(F32)
    prev = ccp_ref[...].astype(F32) * cxp_ref[...].astype(F32)
    prev = jnp.where((i * tm) % seq_len == 0, 0.0, prev)
    last = prev.shape[0] - 1
    p1 = prev[last:last + 1, :]
    p2 = prev[last - 1:last, :]
    rows = lax.broadcasted_iota(I32, (tm, 1), 0)
    u1 = jnp.where(rows == 0, p1, pltpu.roll(u, 1, axis=0))
    u2 = jnp.where(rows == 0, p2, jnp.where(rows == 1, p1, pltpu.roll(u, 2, axis=0)))
    conv = cw_ref[0:1, :] * u2 + cw_ref[1:2, :] * u1 + cw_ref[2:3, :] * u
    y_conv = _dot((cb_ref[...].astype(F32) * conv).astype(BF16), wbc_ref[...])
    y_attn = _dot(o_ref[...], wba_ref[...])
    merged = (jax.nn.sigmoid(gc_ref[...].astype(F32)) * y_conv
              + jax.nn.sigmoid(ga_ref[...].astype(F32)) * y_attn)
    x1 = x_ref[...] + _dot(merged.astype(BF16), wout_ref[...])
    x1_ref[...] = x1
    h2 = _rms(x1, g2_ref[...])
    h2_ref[...] = h2

    h_hi = h2.astype(BF16)
    h_lo = (h2 - h_hi.astype(F32)).astype(BF16)
    logits = (_dot_nt(wrh_ref[...], h_hi) + _dot_nt(wrh_ref[...], h_lo)
              + _dot_nt(wrl_ref[...], h_hi)) + br_ref[...]
    e_iota = lax.broadcasted_iota(I32, logits.shape, 0)
    work = logits
    vals, idxs = [], []
    for _ in range(TOP_K):
        mx = jnp.max(work, axis=0, keepdims=True)
        ix = jnp.min(jnp.where(work == mx, e_iota, N_EXPERTS), axis=0, keepdims=True)
        vals.append(mx)
        idxs.append(ix)
        work = jnp.where(e_iota == ix, -jnp.inf, work)
    ex = [jnp.exp(v - vals[0]) for v in vals]
    tot = ex[0] + ex[1] + ex[2] + ex[3]
    idx_ref[...] = jnp.concatenate(idxs, axis=0)
    gate_ref[...] = jnp.concatenate([e / tot for e in ex], axis=0)


def _post(proj, o_attn, x2, conv_w, wbc, wba, wout, g2, wr_hi, wr_lo, br, tm, seq_len):
    T, D = x2.shape
    prev_rows = 16
    kern = functools.partial(_post_kernel, seq_len=seq_len)

    def colblk(c):
        return pl.BlockSpec((tm, D), lambda i, c=c: (i, c))

    def prevblk(c):
        return pl.BlockSpec((prev_rows, D),
                            lambda i, c=c: (jnp.maximum(i * (tm // prev_rows) - 1, 0), c))

    def whole(a):
        return pl.BlockSpec(a.shape, lambda i, nd=a.ndim: (0,) * nd)

    return pl.pallas_call(
        kern,
        grid=(T // tm,),
        in_specs=[colblk(0), colblk(1), colblk(2), colblk(3), colblk(4), prevblk(1), prevblk(2),
                  pl.BlockSpec((tm, o_attn.shape[1]), lambda i: (i, 0)),
                  pl.BlockSpec((tm, D), lambda i: (i, 0)),
                  whole(conv_w), whole(wbc), whole(wba), whole(wout), whole(g2),
                  whole(wr_hi), whole(wr_lo), whole(br)],
        out_specs=[pl.BlockSpec((tm, D), lambda i: (i, 0)),
                   pl.BlockSpec((tm, D), lambda i: (i, 0)),
                   pl.BlockSpec((TOP_K, tm), lambda i: (0, i)),
                   pl.BlockSpec((TOP_K, tm), lambda i: (0, i))],
        out_shape=[jax.ShapeDtypeStruct((T, D), F32),
                   jax.ShapeDtypeStruct((T, D), F32),
                   jax.ShapeDtypeStruct((TOP_K, T), I32),
                   jax.ShapeDtypeStruct((TOP_K, T), F32)],
        compiler_params=pltpu.CompilerParams(
            dimension_semantics=("arbitrary",), vmem_limit_bytes=VMEM_LIMIT),
        name="post",
    )(proj, proj, proj, proj, proj, proj, proj, o_attn, x2, conv_w, wbc, wba, wout, g2,
      wr_hi, wr_lo, br)


def _route_kernel(idx_ref, dest_ref, be_ref, ends_ref, rank_s, *, bm, chunk):
    T = idx_ref.shape[1]
    e_iota = lax.broadcasted_iota(I32, (N_EXPERTS, chunk), 0)
    upper = (lax.broadcasted_iota(I32, (chunk, chunk), 0)
             < lax.broadcasted_iota(I32, (chunk, chunk), 1)).astype(BF16)

    def onehots(c):
        off = pl.multiple_of(c * chunk, chunk)
        idc = idx_ref[:, pl.ds(off, chunk)]
        return off, [e_iota == idc[k:k + 1, :] for k in range(TOP_K)]

    def pass1(c, carry):
        off, oh = onehots(c)
        member = oh[0].astype(F32) + oh[1].astype(F32) + oh[2].astype(F32) + oh[3].astype(F32)
        before = _dot(member.astype(BF16), upper) + carry
        ranks = [jnp.sum(jnp.where(o, before, 0.0), axis=0, keepdims=True) for o in oh]
        rank_s[:, pl.ds(off, chunk)] = jnp.concatenate(ranks, axis=0)
        return carry + jnp.sum(member, axis=1, keepdims=True)

    counts = lax.fori_loop(0, T // chunk, pass1, jnp.zeros((N_EXPERTS, 1), F32))

    nb = jnp.floor((counts + (bm - 1)) * (1.0 / bm))
    lower = (lax.broadcasted_iota(I32, (N_EXPERTS, N_EXPERTS), 1)
             < lax.broadcasted_iota(I32, (N_EXPERTS, N_EXPERTS), 0)).astype(BF16)
    boff = _dot(lower, jnp.broadcast_to(nb, (N_EXPERTS, LANES)).astype(BF16))[:, 0:1]
    poff = boff * bm
    bend = boff + nb

    def pass2(c, carry):
        off, oh = onehots(c)
        offs = [jnp.sum(jnp.where(o, poff, 0.0), axis=0, keepdims=True) for o in oh]
        dest = jnp.concatenate(offs, axis=0) + rank_s[:, pl.ds(off, chunk)]
        dest_ref[:, pl.ds(off, chunk)] = dest.astype(I32)
        return carry

    lax.fori_loop(0, T // chunk, pass2, 0)

    nbp = be_ref.shape[1]
    b_iota = lax.broadcasted_iota(I32, (N_EXPERTS, nbp), 1).astype(F32)
    be = jnp.sum((bend <= b_iota).astype(F32), axis=0, keepdims=True)
    be_ref[...] = jnp.minimum(be, N_EXPERTS - 1).astype(I32)
    diag = (lax.broadcasted_iota(I32, (N_EXPERTS, LANES), 0)
            == lax.broadcasted_iota(I32, (N_EXPERTS, LANES), 1))
    ends_ref[...] = jnp.sum(jnp.where(diag, bend * bm, 0.0), axis=0, keepdims=True).astype(I32)


def _route(idx_t, bm, nblk_pad):
    K, T = idx_t.shape
    kern = functools.partial(_route_kernel, bm=bm, chunk=512)
    return pl.pallas_call(
        kern,
        out_shape=[jax.ShapeDtypeStruct((K, T), I32),
                   jax.ShapeDtypeStruct((1, nblk_pad), I32),
                   jax.ShapeDtypeStruct((1, LANES), I32)],
        scratch_shapes=[pltpu.VMEM((K, T), F32)],
        compiler_params=pltpu.CompilerParams(vmem_limit_bytes=VMEM_LIMIT),
        name="route",
    )(idx_t)


def _row_copy(src_ref, src_row, dst_ref, dst_row, sem):
    return pltpu.make_async_copy(src_ref.at[pl.ds(src_row, 1), :], dst_ref.at[pl.ds(dst_row, 1), :], sem)


def _dispatch_kernel(ends_ref, dest_ref, h_ref, xs_ref, zbuf, sem, zsem):
    tt = h_ref.shape[0]
    bm = zbuf.shape[0]

    @pl.when(pl.program_id(0) == 0)
    def _():
        zbuf[...] = jnp.zeros(zbuf.shape, F32)
        fills = []
        for e in range(N_EXPERTS):
            end = ends_ref[e]
            start = ends_ref[e - 1] if e > 0 else 0
            fills.append((end > start,
                          pltpu.make_async_copy(
                              zbuf, xs_ref.at[pl.ds(pl.multiple_of(end - bm, bm), bm), :], zsem)))
        for nonempty, fill in fills:
            pl.when(nonempty)(fill.start)
        for nonempty, fill in fills:
            pl.when(nonempty)(fill.wait)

        def tail(b):
            return pltpu.make_async_copy(
                zbuf, xs_ref.at[pl.ds(pl.multiple_of(b * bm, bm), bm), :], zsem)

        first_unused = ends_ref[N_EXPERTS - 1] // bm
        n_blocks = xs_ref.shape[0] // bm
        lax.fori_loop(first_unused, n_blocks, lambda b, c: (tail(b).start(), c)[1], 0)
        lax.fori_loop(first_unused, n_blocks, lambda b, c: (tail(b).wait(), c)[1], 0)

    def issue(i, carry):
        base = pl.multiple_of(i * ISSUE_GROUP, ISSUE_GROUP)
        for j in range(ISSUE_GROUP):
            for k in range(TOP_K):
                _row_copy(h_ref, base + j, xs_ref, dest_ref[k, base + j], sem).start(priority=k % 2)
        return carry

    lax.fori_loop(0, tt // ISSUE_GROUP, issue, 0)

    def drain(t, carry):
        for k in range(TOP_K):
            _row_copy(h_ref, t, xs_ref, dest_ref[k, t], sem).wait()
        return carry

    lax.fori_loop(0, tt, drain, 0)


def _dispatch(ends, dest3, h2, n_rows, bm):
    nt, K, tt = dest3.shape
    T, D = h2.shape
    grid_spec = pltpu.PrefetchScalarGridSpec(
        num_scalar_prefetch=1,
        grid=(nt,),
        in_specs=[pl.BlockSpec((None, K, tt), lambda i, ends_r: (i, 0, 0), memory_space=pltpu.SMEM),
                  pl.BlockSpec((tt, D), lambda i, ends_r: (i, 0))],
        out_specs=pl.BlockSpec(memory_space=pl.ANY),
        scratch_shapes=[pltpu.VMEM((bm, D), F32), pltpu.SemaphoreType.DMA(()),
                        pltpu.SemaphoreType.DMA(())],
    )
    return pl.pallas_call(
        _dispatch_kernel,
        grid_spec=grid_spec,
        out_shape=jax.ShapeDtypeStruct((n_rows, D), F32),
        compiler_params=pltpu.CompilerParams(
            dimension_semantics=("arbitrary",), vmem_limit_bytes=VMEM_LIMIT),
        name="dispatch",
    )(ends, dest3, h2)


def _expert_kernel(be_ref, nu_ref, xs_ref, wgu_ref, bgu_ref, wd_ref, bd_ref, ys_ref,
                   wgu_s, wd_s, *, chunk):
    b = pl.program_id(0)
    d_exp = wd_ref.shape[0]
    active = b < nu_ref[0]
    new_expert = jnp.logical_or(b == 0, be_ref[b] != be_ref[jnp.maximum(b - 1, 0)])

    @pl.when(jnp.logical_and(active, new_expert))
    def _():
        wgu_s[...] = wgu_ref[...].astype(BF16)
        wd_s[...] = wd_ref[...].astype(BF16)

    @pl.when(active)
    def _():
        x = xs_ref[...].astype(BF16)
        y = jnp.zeros(ys_ref.shape, F32)
        for c in range(d_exp // chunk):
            lo = c * chunk
            gate = _dot(x, wgu_s[:, lo:lo + chunk]) + bgu_ref[:, lo:lo + chunk]
            up = _dot(x, wgu_s[:, d_exp + lo:d_exp + lo + chunk]) + bgu_ref[:, d_exp + lo:d_exp + lo + chunk]
            gate = jnp.minimum(gate, SWIGLU_LIMIT)
            up = jnp.clip(up, -SWIGLU_LIMIT, SWIGLU_LIMIT)
            act = (up + 1.0) * (gate * jax.nn.sigmoid(SWIGLU_ALPHA * gate))
            y = y + _dot(act.astype(BF16), wd_s[lo:lo + chunk, :])
        ys_ref[...] = y + bd_ref[...]

    @pl.when(jnp.logical_not(active))
    def _():
        ys_ref[...] = jnp.zeros(ys_ref.shape, F32)


def _experts(be, nused, xs, wgu, bgu, wd, bd, bm):
    P, D = xs.shape
    E, _, two_de = wgu.shape
    d_exp = wd.shape[1]
    kern = functools.partial(_expert_kernel, chunk=512)

    def rows(b, be_r, nu_r):
        return (jnp.minimum(b, nu_r[0] - 1), 0)

    def per_expert(b, be_r, nu_r):
        return (be_r[b], 0, 0)

    grid_spec = pltpu.PrefetchScalarGridSpec(
        num_scalar_prefetch=2,
        grid=(P // bm,),
        in_specs=[pl.BlockSpec((bm, D), rows),
                  pl.BlockSpec((None, D, two_de), per_expert),
                  pl.BlockSpec((None, 1, two_de), per_expert),
                  pl.BlockSpec((None, d_exp, D), per_expert),
                  pl.BlockSpec((None, 1, D), per_expert)],
        out_specs=pl.BlockSpec((bm, D), lambda b, be_r, nu_r: (b, 0)),
        scratch_shapes=[pltpu.VMEM((D, two_de), BF16), pltpu.VMEM((d_exp, D), BF16)],
    )
    return pl.pallas_call(
        kern,
        grid_spec=grid_spec,
        out_shape=jax.ShapeDtypeStruct((P, D), F32),
        compiler_params=pltpu.CompilerParams(
            dimension_semantics=("arbitrary",), vmem_limit_bytes=VMEM_LIMIT),
        name="experts",
    )(be, nused, xs, wgu, bgu, wd, bd)


def _combine_kernel(dest_ref, x1_ref, gates_ref, ys_ref, out_ref, buf, sem):
    tt = x1_ref.shape[0]

    def issue(i, carry):
        base = pl.multiple_of(i * ISSUE_GROUP, ISSUE_GROUP)
        for j in range(ISSUE_GROUP):
            for k in range(TOP_K):
                _row_copy(ys_ref, dest_ref[k, base + j], buf.at[k], base + j, sem).start(priority=k % 2)
        return carry

    lax.fori_loop(0, tt // ISSUE_GROUP, issue, 0)

    def drain(t, carry):
        for k in range(TOP_K):
            _row_copy(ys_ref, dest_ref[k, t], buf.at[k], t, sem).wait()
        return carry

    lax.fori_loop(0, tt, drain, 0)

    acc = x1_ref[...]
    for k in range(TOP_K):
        acc = acc + gates_ref[:, k:k + 1] * buf[k]
    out_ref[...] = acc


def _combine(dest3, x1, gates, ys):
    nt, K, tt = dest3.shape
    T, D = x1.shape
    return pl.pallas_call(
        _combine_kernel,
        grid=(nt,),
        in_specs=[pl.BlockSpec((None, K, tt), lambda i: (i, 0, 0), memory_space=pltpu.SMEM),
                  pl.BlockSpec((tt, D), lambda i: (i, 0)),
                  pl.BlockSpec((tt, K), lambda i: (i, 0)),
                  pl.BlockSpec(memory_space=pl.ANY)],
        out_specs=pl.BlockSpec((tt, D), lambda i: (i, 0)),
        out_shape=jax.ShapeDtypeStruct((T, D), F32),
        scratch_shapes=[pltpu.VMEM((K, tt, D), F32), pltpu.SemaphoreType.DMA(())],
        compiler_params=pltpu.CompilerParams(
            dimension_semantics=("arbitrary",), vmem_limit_bytes=VMEM_LIMIT),
        name="combine",
    )(dest3, x1, gates, ys)


def _layer(x2, B, S, norm1_g, w_in, q_norm_g, k_norm_g, rel_bias, conv_w, w_branch_conv,
           w_branch_attn, w_out, norm2_g, w_router, b_router, w_gate_up, b_gate_up, w_down, b_down):
    T, D = x2.shape
    attn_w = N_GROUPS * HEADS_PER_GROUP * HEAD_DIM
    cuts = np.cumsum([0, D, D, D, attn_w, attn_w, attn_w, D, D])
    order = [0, 1, 2, 6, 7, 3, 4, 5]
    w_in_r = jnp.concatenate([w_in[:, cuts[s]:cuts[s + 1]] for s in order], axis=1).astype(BF16)
    q_col = 5 * D // LANES
    k_col = q_col + attn_w // LANES
    v_col = k_col + attn_w // LANES

    proj = _inproj(x2, norm1_g.reshape(1, D), w_in_r, tm=512, tn=w_in_r.shape[1] // 2)

    gq2 = jnp.tile(q_norm_g.reshape(1, HEAD_DIM), (1, LANES // HEAD_DIM))
    gk2 = jnp.tile(k_norm_g.reshape(1, HEAD_DIM), (1, LANES // HEAD_DIM))
    o_attn = _attention(proj, gq2, gk2, _attn_bias_tables(rel_bias), B, S, q_col, k_col, v_col)

    wr_hi = w_router.T.astype(BF16)
    wr_lo = (w_router.T - wr_hi.astype(F32)).astype(BF16)
    x1, h2, idx_t, gates_t = _post(
        proj, o_attn, x2, conv_w, w_branch_conv.astype(BF16), w_branch_attn.astype(BF16),
        w_out.astype(BF16), norm2_g.reshape(1, D), wr_hi, wr_lo, b_router.reshape(N_EXPERTS, 1),
        tm=512, seq_len=S)

    bm = MOE_BM
    nblk = T * TOP_K // bm + N_EXPERTS
    nblk_pad = -(-nblk // LANES) * LANES
    dest, be, ends = _route(idx_t, bm, nblk_pad)
    ends = ends.reshape(LANES)
    nused = ends[N_EXPERTS - 1:N_EXPERTS] // bm
    tt = ROW_TILE
    dest3 = dest.reshape(TOP_K, T // tt, tt).transpose(1, 0, 2)

    xs = _dispatch(ends, dest3, h2, nblk * bm, bm)
    ys = _experts(be.reshape(nblk_pad), nused, xs,
                  w_gate_up, b_gate_up.reshape(N_EXPERTS, 1, -1),
                  w_down, b_down.reshape(N_EXPERTS, 1, -1), bm)
    return _combine(dest3, x1, gates_t.T, ys)


def kernel(x, norm1_g, w_in, q_norm_g, k_norm_g, rel_bias, conv_w, w_branch_conv, w_branch_attn, w_out, norm2_g, w_router, b_router, w_gate_up, b_gate_up, w_down, b_down):
    B, S, D = x.shape
    x2 = x.reshape(B * S, D)
    for l in range(norm1_g.shape[0]):
        x2 = _layer(x2, B, S, norm1_g[l], w_in[l], q_norm_g[l], k_norm_g[l], rel_bias, conv_w[l],
                    w_branch_conv[l], w_branch_attn[l], w_out[l], norm2_g[l], w_router[l],
                    b_router[l], w_gate_up[l], b_gate_up[l], w_down[l], b_down[l])
    return x2.reshape(B, S, D)
```

```python
import functools
import math

import numpy as np
import jax
import jax.numpy as jnp
from jax import lax
from jax.experimental import pallas as pl
from jax.experimental.pallas import tpu as pltpu

F32 = jnp.float32
BF16 = jnp.bfloat16
I32 = jnp.int32

HEAD_DIM = 64
ATTN_GROUPS = ((128, 1), (512, 4), (2048, 16))
HEADS_PER_GROUP = 4
N_GROUPS = len(ATTN_GROUPS)
ATTN_BLK = 128
N_BUCKETS = 32
MAX_DISTANCE = 2048
CONV_K = 3
N_EXPERTS = 32
TOP_K = 4
SWIGLU_LIMIT = 7.0
SWIGLU_ALPHA = 1.702
EPS = 1e-6
MASK_VALUE = -1e30

LANES = 128
VMEM_LIMIT = 52 * 1024 * 1024

MOE_BM = 512
ROW_TILE = 512
ISSUE_GROUP = 8


def _rms(x, gain):
    return x * lax.rsqrt(jnp.mean(x * x, axis=-1, keepdims=True) + EPS) * gain


def _dot(a, b):
    return jnp.dot(a, b, preferred_element_type=F32)


def _dot_nt(a, b):
    return lax.dot_general(a, b, (((1,), (1,)), ((), ())), preferred_element_type=F32)


def _inproj_kernel(x_ref, g_ref, w_ref, o_ref):
    h = _rms(x_ref[...], g_ref[...]).astype(BF16)
    o_ref[...] = _dot(h, w_ref[...]).astype(BF16)


def _inproj(x2, g1, w_in_bf, tm, tn):
    T, D = x2.shape
    N = w_in_bf.shape[1]
    return pl.pallas_call(
        _inproj_kernel,
        grid=(N // tn, T // tm),
        in_specs=[
            pl.BlockSpec((tm, D), lambda j, i: (i, 0)),
            pl.BlockSpec((1, D), lambda j, i: (0, 0)),
            pl.BlockSpec((D, tn), lambda j, i: (0, j)),
        ],
        out_specs=pl.BlockSpec((tm, tn), lambda j, i: (i, j)),
        out_shape=jax.ShapeDtypeStruct((T, N), BF16),
        compiler_params=pltpu.CompilerParams(
            dimension_semantics=("arbitrary", "arbitrary"), vmem_limit_bytes=VMEM_LIMIT),
        name="inproj",
    )(x2, g1, w_in_bf)


def _t5_bucket_np(dist):
    max_exact = N_BUCKETS // 2
    d = np.maximum(dist.astype(np.float64), 1.0)
    large = max_exact + (np.log(d / max_exact) / math.log(MAX_DISTANCE / max_exact)
                         * (N_BUCKETS - max_exact)).astype(np.int32)
    large = np.minimum(large, N_BUCKETS - 1)
    return np.where(dist < max_exact, dist, large)


def _attn_bias_tables(rel_bias):
    blk = ATTN_BLK
    qi = np.arange(blk)[:, None]
    kj = np.arange(2 * blk)[None, :]
    delta = qi + blk - kj
    band = (delta >= 0) & (delta <= blk)
    first = band & (kj >= blk)
    tables = []
    for gi, (_, dil) in enumerate(ATTN_GROUPS):
        bucket = _t5_bucket_np(np.clip(delta, 0, blk) * dil)
        hs = slice(gi * HEADS_PER_GROUP, (gi + 1) * HEADS_PER_GROUP)
        onehot = (jnp.asarray(bucket, I32)[..., None] == jnp.arange(N_BUCKETS, dtype=I32)).astype(F32)
        bias = jnp.einsum('qkb,bh->hqk', onehot, rel_bias[:, hs].astype(F32),
                          precision=lax.Precision.HIGHEST)
        rest = jnp.where(band[None], bias, MASK_VALUE)
        frst = jnp.where(first[None], bias, MASK_VALUE)
        tables.append(jnp.stack([rest, frst], axis=1))
    return jnp.stack(tables, axis=0)


def _attn_kernel(q0, q1, q2, k0, k1, k2, v0, v1, v2, gq_ref, gk_ref, bias_ref, o_ref,
                 qa, qb, ks, va, vb, acc_s, m_s, l_s):
    S = q0.shape[0]
    pad = ks.shape[0] - S
    blk = ATTN_BLK
    hd = HEAD_DIM
    q_refs, k_refs, v_refs = (q0, q1, q2), (k0, k1, k2), (v0, v1, v2)
    head_a = lax.broadcasted_iota(I32, (1, LANES), 1) < hd
    same_head = (lax.broadcasted_iota(I32, (LANES, LANES), 0) // hd
                 == lax.broadcasted_iota(I32, (LANES, LANES), 1) // hd).astype(BF16)

    zeros = jnp.zeros((pad, LANES), F32)
    ks[0:pad, :] = zeros
    va[0:pad, :] = zeros
    vb[0:pad, :] = zeros

    def head_norm(x, gain):
        sq = x * x
        hi = sq.astype(BF16)
        lo = (sq - hi.astype(F32)).astype(BF16)
        ss = _dot(hi, same_head) + _dot(lo, same_head)
        return x * lax.rsqrt(ss * (1.0 / hd) + EPS) * gain

    for g, (window, dil) in enumerate(ATTN_GROUPS):
        span = blk * dil
        nblk = S // span
        stride = dil if dil > 1 else None
        qn = head_norm(q_refs[g][...].astype(F32), gq_ref[...]) * (hd ** -0.5)
        qa[...] = jnp.where(head_a, qn, 0.0)
        qb[...] = jnp.where(head_a, 0.0, qn)
        ks[pad:pad + S, :] = head_norm(k_refs[g][...].astype(F32), gk_ref[...])
        v = v_refs[g][...].astype(F32)
        va[pad:pad + S, :] = jnp.where(head_a, v, 1.0)
        vb[pad:pad + S, :] = jnp.where(head_a, 1.0, v)

        def body(idx, carry, g=g, span=span, nblk=nblk, stride=stride):
            r = idx // nblk
            n = idx % nblk
            qstart = r + n * span
            rows = pl.ds(qstart, blk, stride=stride)
            win = pl.ds(pad + qstart - span, 2 * blk, stride=stride)
            first = jnp.where(n == 0, 1, 0)
            kw = ks[win, :].astype(BF16)
            res, ms = [], []
            for hh, (q_ref, v_ref) in enumerate(((qa, va), (qb, vb))):
                s = _dot_nt(q_ref[rows, :].astype(BF16), kw) + bias_ref[g, hh, first]
                m = jnp.max(s, axis=-1, keepdims=True)
                p = jnp.exp(s - m).astype(BF16)
                res.append(_dot(p, v_ref[win, :].astype(BF16)))
                ms.append(m)
            acc_s[g, rows, :] = jnp.where(head_a, res[0], res[1])
            l_s[g, rows, :] = pltpu.roll(jnp.where(head_a, res[1], res[0]), hd, axis=1)
            m_s[g, rows, :] = jnp.where(head_a, ms[0], ms[1])
            return carry

        lax.fori_loop(0, S // blk, body, 0, unroll=8)

    m_all = jnp.maximum(jnp.maximum(m_s[0], m_s[1]), m_s[2])
    num = jnp.zeros((S, LANES), F32)
    den = jnp.zeros((S, LANES), F32)
    for g in range(N_GROUPS):
        w = jnp.exp(m_s[g] - m_all)
        num = num + w * acc_s[g]
        den = den + w * l_s[g]
    o_ref[...] = (num / den).astype(BF16)


def _attention(proj, gq2, gk2, bias_tab, B, S, q_col, k_col, v_col):
    T = proj.shape[0]
    n_pairs = HEADS_PER_GROUP // 2

    def col_spec(base, g):
        return pl.BlockSpec((S, LANES), lambda b, p, base=base, g=g: (b, base + n_pairs * g + p))

    in_specs = ([col_spec(q_col, g) for g in range(N_GROUPS)]
                + [col_spec(k_col, g) for g in range(N_GROUPS)]
                + [col_spec(v_col, g) for g in range(N_GROUPS)]
                + [pl.BlockSpec((1, LANES), lambda b, p: (0, 0)),
                   pl.BlockSpec((1, LANES), lambda b, p: (0, 0)),
                   pl.BlockSpec((N_GROUPS, 2, 2, ATTN_BLK, 2 * ATTN_BLK), lambda b, p: (0, p, 0, 0, 0))])
    pad = ATTN_BLK * max(d for _, d in ATTN_GROUPS)
    return pl.pallas_call(
        _attn_kernel,
        grid=(B, n_pairs),
        in_specs=in_specs,
        out_specs=pl.BlockSpec((S, LANES), lambda b, p: (b, p)),
        out_shape=jax.ShapeDtypeStruct((T, HEADS_PER_GROUP * HEAD_DIM), BF16),
        scratch_shapes=[
            pltpu.VMEM((S, LANES), F32),
            pltpu.VMEM((S, LANES), F32),
            pltpu.VMEM((pad + S, LANES), F32),
            pltpu.VMEM((pad + S, LANES), F32),
            pltpu.VMEM((pad + S, LANES), F32),
            pltpu.VMEM((N_GROUPS, S, LANES), F32),
            pltpu.VMEM((N_GROUPS, S, LANES), F32),
            pltpu.VMEM((N_GROUPS, S, LANES), F32),
        ],
        compiler_params=pltpu.CompilerParams(
            dimension_semantics=("arbitrary", "arbitrary"), vmem_limit_bytes=VMEM_LIMIT),
        name="attn",
    )(*([proj] * 9), gq2, gk2, bias_tab)


def _post_kernel(cb_ref, cc_ref, cx_ref, gc_ref, ga_ref, ccp_ref, cxp_ref, o_ref, x_ref, cw_ref,
                 wbc_ref, wba_ref, wout_ref, g2_ref, wrh_ref, wrl_ref, br_ref,
                 x1_ref, h2_ref, idx_ref, gate_ref, *, seq_len):
    i = pl.program_id(0)
    tm = x_ref.shape[0]
    u = cc_ref[...].astype(F32) * cx_ref[...].astype(F32)
    prev = ccp_ref[...].astype(F32) * cxp_ref[...].astype(F32)
    prev = jnp.where((i * tm) % seq_len == 0, 0.0, prev)
    last = prev.shape[0] - 1
    p1 = prev[last:last + 1, :]
    p2 = prev[last - 1:last, :]
    rows = lax.broadcasted_iota(I32, (tm, 1), 0)
    u1 = jnp.where(rows == 0, p1, pltpu.roll(u, 1, axis=0))
    u2 = jnp.where(rows == 0, p2, jnp.where(rows == 1, p1, pltpu.roll(u, 2, axis=0)))
    conv = cw_ref[0:1, :] * u2 + cw_ref[1:2, :] * u1 + cw_ref[2:3, :] * u
    y_conv = _dot((cb_ref[...].astype(F32) * conv).astype(BF16), wbc_ref[...])
    y_attn = _dot(o_ref[...], wba_ref[...])
    merged = (jax.nn.sigmoid(gc_ref[...].astype(F32)) * y_conv
              + jax.nn.sigmoid(ga_ref[...].astype(F32)) * y_attn)
    x1 = x_ref[...] + _dot(merged.astype(BF16), wout_ref[...])
    x1_ref[...] = x1
    h2 = _rms(x1, g2_ref[...])
    h2_ref[...] = h2

    h_hi = h2.astype(BF16)
    h_lo = (h2 - h_hi.astype(F32)).astype(BF16)
    logits = (_dot_nt(wrh_ref[...], h_hi) + _dot_nt(wrh_ref[...], h_lo)
              + _dot_nt(wrl_ref[...], h_hi)) + br_ref[...]
    e_iota = lax.broadcasted_iota(I32, logits.shape, 0)
    work = logits
    vals, idxs = [], []
    for _ in range(TOP_K):
        mx = jnp.max(work, axis=0, keepdims=True)
        ix = jnp.min(jnp.where(work == mx, e_iota, N_EXPERTS), axis=0, keepdims=True)
        vals.append(mx)
        idxs.append(ix)
        work = jnp.where(e_iota == ix, -jnp.inf, work)
    ex = [jnp.exp(v - vals[0]) for v in vals]
    tot = ex[0] + ex[1] + ex[2] + ex[3]
    idx_ref[...] = jnp.concatenate(idxs, axis=0)
    gate_ref[...] = jnp.concatenate([e / tot for e in ex], axis=0)


def _post(proj, o_attn, x2, conv_w, wbc, wba, wout, g2, wr_hi, wr_lo, br, tm, seq_len):
    T, D = x2.shape
    prev_rows = 16
    kern = functools.partial(_post_kernel, seq_len=seq_len)

    def colblk(c):
        return pl.BlockSpec((tm, D), lambda i, c=c: (i, c))

    def prevblk(c):
        return pl.BlockSpec((prev_rows, D),
                            lambda i, c=c: (jnp.maximum(i * (tm // prev_rows) - 1, 0), c))

    def whole(a):
        return pl.BlockSpec(a.shape, lambda i, nd=a.ndim: (0,) * nd)

    return pl.pallas_call(
        kern,
        grid=(T // tm,),
        in_specs=[colblk(0), colblk(1), colblk(2), colblk(3), colblk(4), prevblk(1), prevblk(2),
                  pl.BlockSpec((tm, o_attn.shape[1]), lambda i: (i, 0)),
                  pl.BlockSpec((tm, D), lambda i: (i, 0)),
                  whole(conv_w), whole(wbc), whole(wba), whole(wout), whole(g2),
                  whole(wr_hi), whole(wr_lo), whole(br)],
        out_specs=[pl.BlockSpec((tm, D), lambda i: (i, 0)),
                   pl.BlockSpec((tm, D), lambda i: (i, 0)),
                   pl.BlockSpec((TOP_K, tm), lambda i: (0, i)),
                   pl.BlockSpec((TOP_K, tm), lambda i: (0, i))],
        out_shape=[jax.ShapeDtypeStruct((T, D), F32),
                   jax.ShapeDtypeStruct((T, D), F32),
                   jax.ShapeDtypeStruct((TOP_K, T), I32),
                   jax.ShapeDtypeStruct((TOP_K, T), F32)],
        compiler_params=pltpu.CompilerParams(
            dimension_semantics=("arbitrary",), vmem_limit_bytes=VMEM_LIMIT),
        name="post",
    )(proj, proj, proj, proj, proj, proj, proj, o_attn, x2, conv_w, wbc, wba, wout, g2,
      wr_hi, wr_lo, br)


def _route_kernel(idx_ref, dest_ref, be_ref, ends_ref, rank_s, *, bm, chunk):
    T = idx_ref.shape[1]
    e_iota = lax.broadcasted_iota(I32, (N_EXPERTS, chunk), 0)
    upper = (lax.broadcasted_iota(I32, (chunk, chunk), 0)
             < lax.broadcasted_iota(I32, (chunk, chunk), 1)).astype(BF16)

    def onehots(c):
        off = pl.multiple_of(c * chunk, chunk)
        idc = idx_ref[:, pl.ds(off, chunk)]
        return off, [e_iota == idc[k:k + 1, :] for k in range(TOP_K)]

    def pass1(c, carry):
        off, oh = onehots(c)
        member = oh[0].astype(F32) + oh[1].astype(F32) + oh[2].astype(F32) + oh[3].astype(F32)
        before = _dot(member.astype(BF16), upper) + carry
        ranks = [jnp.sum(jnp.where(o, before, 0.0), axis=0, keepdims=True) for o in oh]
        rank_s[:, pl.ds(off, chunk)] = jnp.concatenate(ranks, axis=0)
        return carry + jnp.sum(member, axis=1, keepdims=True)

    counts = lax.fori_loop(0, T // chunk, pass1, jnp.zeros((N_EXPERTS, 1), F32))

    nb = jnp.floor((counts + (bm - 1)) * (1.0 / bm))
    lower = (lax.broadcasted_iota(I32, (N_EXPERTS, N_EXPERTS), 1)
             < lax.broadcasted_iota(I32, (N_EXPERTS, N_EXPERTS), 0)).astype(BF16)
    boff = _dot(lower, jnp.broadcast_to(nb, (N_EXPERTS, LANES)).astype(BF16))[:, 0:1]
    poff = boff * bm
    bend = boff + nb

    def pass2(c, carry):
        off, oh = onehots(c)
        offs = [jnp.sum(jnp.where(o, poff, 0.0), axis=0, keepdims=True) for o in oh]
        dest = jnp.concatenate(offs, axis=0) + rank_s[:, pl.ds(off, chunk)]
        dest_ref[:, pl.ds(off, chunk)] = dest.astype(I32)
        return carry

    lax.fori_loop(0, T // chunk, pass2, 0)

    nbp = be_ref.shape[1]
    b_iota = lax.broadcasted_iota(I32, (N_EXPERTS, nbp), 1).astype(F32)
    be = jnp.sum((bend <= b_iota).astype(F32), axis=0, keepdims=True)
    be_ref[...] = jnp.minimum(be, N_EXPERTS - 1).astype(I32)
    diag = (lax.broadcasted_iota(I32, (N_EXPERTS, LANES), 0)
            == lax.broadcasted_iota(I32, (N_EXPERTS, LANES), 1))
    ends_ref[...] = jnp.sum(jnp.where(diag, bend * bm, 0.0), axis=0, keepdims=True).astype(I32)


def _route(idx_t, bm, nblk_pad):
    K, T = idx_t.shape
    kern = functools.partial(_route_kernel, bm=bm, chunk=512)
    return pl.pallas_call(
        kern,
        out_shape=[jax.ShapeDtypeStruct((K, T), I32),
                   jax.ShapeDtypeStruct((1, nblk_pad), I32),
                   jax.ShapeDtypeStruct((1, LANES), I32)],
        scratch_shapes=[pltpu.VMEM((K, T), F32)],
        compiler_params=pltpu.CompilerParams(vmem_limit_bytes=VMEM_LIMIT),
        name="route",
    )(idx_t)


def _row_copy(src_ref, src_row, dst_ref, dst_row, sem):
    return pltpu.make_async_copy(src_ref.at[pl.ds(src_row, 1), :], dst_ref.at[pl.ds(dst_row, 1), :], sem)


def _dispatch_kernel(ends_ref, dest_ref, h_ref, xs_ref, zbuf, sem, zsem):
    tt = h_ref.shape[0]
    bm = zbuf.shape[0]

    @pl.when(pl.program_id(0) == 0)
    def _():
        zbuf[...] = jnp.zeros(zbuf.shape, F32)
        fills = []
        for e in range(N_EXPERTS):
            end = ends_ref[e]
            start = ends_ref[e - 1] if e > 0 else 0
            fills.append((end > start,
                          pltpu.make_async_copy(
                              zbuf, xs_ref.at[pl.ds(pl.multiple_of(end - bm, bm), bm), :], zsem)))
        for nonempty, fill in fills:
            pl.when(nonempty)(fill.start)
        for nonempty, fill in fills:
            pl.when(nonempty)(fill.wait)

        def tail(b):
            return pltpu.make_async_copy(
                zbuf, xs_ref.at[pl.ds(pl.multiple_of(b * bm, bm), bm), :], zsem)

        first_unused = ends_ref[N_EXPERTS - 1] // bm
        n_blocks = xs_ref.shape[0] // bm
        lax.fori_loop(first_unused, n_blocks, lambda b, c: (tail(b).start(), c)[1], 0)
        lax.fori_loop(first_unused, n_blocks, lambda b, c: (tail(b).wait(), c)[1], 0)

    def issue(i, carry):
        base = pl.multiple_of(i * ISSUE_GROUP, ISSUE_GROUP)
        slot0 = pl.multiple_of(i * (ISSUE_GROUP * TOP_K), ISSUE_GROUP * TOP_K)
        for j in range(ISSUE_GROUP):
            for k in range(TOP_K):
                _row_copy(h_ref, base + j, xs_ref, dest_ref[0, slot0 + (j * TOP_K + k)],
                          sem).start(priority=k % 2)
        return carry

    lax.fori_loop(0, tt // ISSUE_GROUP, issue, 0)

    def drain(t, carry):
        for k in range(TOP_K):
            _row_copy(h_ref, t, xs_ref, dest_ref[0, t * TOP_K + k], sem).wait()
        return carry

    lax.fori_loop(0, tt, drain, 0)


def _dispatch(ends, dest3, h2, n_rows, bm):
    nt, _, slots = dest3.shape
    tt = slots // TOP_K
    T, D = h2.shape
    grid_spec = pltpu.PrefetchScalarGridSpec(
        num_scalar_prefetch=1,
        grid=(nt,),
        in_specs=[pl.BlockSpec((None, 1, slots), lambda i, ends_r: (i, 0, 0), memory_space=pltpu.SMEM),
                  pl.BlockSpec((tt, D), lambda i, ends_r: (i, 0))],
        out_specs=pl.BlockSpec(memory_space=pl.ANY),
        scratch_shapes=[pltpu.VMEM((bm, D), F32), pltpu.SemaphoreType.DMA(()),
                        pltpu.SemaphoreType.DMA(())],
    )
    return pl.pallas_call(
        _dispatch_kernel,
        grid_spec=grid_spec,
        out_shape=jax.ShapeDtypeStruct((n_rows, D), F32),
        compiler_params=pltpu.CompilerParams(
            dimension_semantics=("arbitrary",), vmem_limit_bytes=VMEM_LIMIT),
        name="dispatch",
    )(ends, dest3, h2)


def _expert_kernel(be_ref, nu_ref, xs_ref, wgu_ref, bgu_ref, wd_ref, bd_ref, ys_ref,
                   wgu_s, wd_s, *, chunk):
    b = pl.program_id(0)
    d_exp = wd_ref.shape[0]
    active = b < nu_ref[0]
    new_expert = jnp.logical_or(b == 0, be_ref[b] != be_ref[jnp.maximum(b - 1, 0)])

    @pl.when(jnp.logical_and(active, new_expert))
    def _():
        wgu_s[...] = wgu_ref[...].astype(BF16)
        wd_s[...] = wd_ref[...].astype(BF16)

    @pl.when(active)
    def _():
        x = xs_ref[...].astype(BF16)
        y = jnp.zeros(ys_ref.shape, F32)
        for c in range(d_exp // chunk):
            lo = c * chunk
            gate = _dot(x, wgu_s[:, lo:lo + chunk]) + bgu_ref[:, lo:lo + chunk]
            up = _dot(x, wgu_s[:, d_exp + lo:d_exp + lo + chunk]) + bgu_ref[:, d_exp + lo:d_exp + lo + chunk]
            gate = jnp.minimum(gate, SWIGLU_LIMIT)
            up = jnp.clip(up, -SWIGLU_LIMIT, SWIGLU_LIMIT)
            act = (up + 1.0) * (gate * jax.nn.sigmoid(SWIGLU_ALPHA * gate))
            y = y + _dot(act.astype(BF16), wd_s[lo:lo + chunk, :])
        ys_ref[...] = y + bd_ref[...]

    @pl.when(jnp.logical_not(active))
    def _():
        ys_ref[...] = jnp.zeros(ys_ref.shape, F32)


def _experts(be, nused, xs, wgu, bgu, wd, bd, bm):
    P, D = xs.shape
    E, _, two_de = wgu.shape
    d_exp = wd.shape[1]
    kern = functools.partial(_expert_kernel, chunk=512)

    def rows(b, be_r, nu_r):
        return (jnp.minimum(b, nu_r[0] - 1), 0)

    def per_expert(b, be_r, nu_r):
        return (be_r[b], 0, 0)

    grid_spec = pltpu.PrefetchScalarGridSpec(
        num_scalar_prefetch=2,
        grid=(P // bm,),
        in_specs=[pl.BlockSpec((bm, D), rows),
                  pl.BlockSpec((None, D, two_de), per_expert),
                  pl.BlockSpec((None, 1, two_de), per_expert),
                  pl.BlockSpec((None, d_exp, D), per_expert),
                  pl.BlockSpec((None, 1, D), per_expert)],
        out_specs=pl.BlockSpec((bm, D), lambda b, be_r, nu_r: (b, 0)),
        scratch_shapes=[pltpu.VMEM((D, two_de), BF16), pltpu.VMEM((d_exp, D), BF16)],
    )
    return pl.pallas_call(
        kern,
        grid_spec=grid_spec,
        out_shape=jax.ShapeDtypeStruct((P, D), F32),
        compiler_params=pltpu.CompilerParams(
            dimension_semantics=("arbitrary",), vmem_limit_bytes=VMEM_LIMIT),
        name="experts",
    )(be, nused, xs, wgu, bgu, wd, bd)


def _combine_kernel(dest_ref, x1_ref, gates_ref, ys_ref, out_ref, buf, sem):
    tt = x1_ref.shape[0]

    def issue(i, carry):
        base = pl.multiple_of(i * ISSUE_GROUP, ISSUE_GROUP)
        slot0 = pl.multiple_of(i * (ISSUE_GROUP * TOP_K), ISSUE_GROUP * TOP_K)
        for j in range(ISSUE_GROUP):
            for k in range(TOP_K):
                _row_copy(ys_ref, dest_ref[0, slot0 + (j * TOP_K + k)], buf.at[k], base + j,
                          sem).start(priority=k % 2)
        return carry

    lax.fori_loop(0, tt // ISSUE_GROUP, issue, 0)

    def drain(t, carry):
        for k in range(TOP_K):
            _row_copy(ys_ref, dest_ref[0, t * TOP_K + k], buf.at[k], t, sem).wait()
        return carry

    lax.fori_loop(0, tt, drain, 0)

    acc = x1_ref[...]
    for k in range(TOP_K):
        acc = acc + gates_ref[:, k:k + 1] * buf[k]
    out_ref[...] = acc


def _combine(dest3, x1, gates, ys):
    nt, _, slots = dest3.shape
    K = TOP_K
    tt = slots // K
    T, D = x1.shape
    return pl.pallas_call(
        _combine_kernel,
        grid=(nt,),
        in_specs=[pl.BlockSpec((None, 1, slots), lambda i: (i, 0, 0), memory_space=pltpu.SMEM),
                  pl.BlockSpec((tt, D), lambda i: (i, 0)),
                  pl.BlockSpec((tt, K), lambda i: (i, 0)),
                  pl.BlockSpec(memory_space=pl.ANY)],
        out_specs=pl.BlockSpec((tt, D), lambda i: (i, 0)),
        out_shape=jax.ShapeDtypeStruct((T, D), F32),
        scratch_shapes=[pltpu.VMEM((K, tt, D), F32), pltpu.SemaphoreType.DMA(())],
        compiler_params=pltpu.CompilerParams(
            dimension_semantics=("arbitrary",), vmem_limit_bytes=VMEM_LIMIT),
        name="combine",
    )(dest3, x1, gates, ys)


def _layer(x2, B, S, norm1_g, w_in, q_norm_g, k_norm_g, rel_bias, conv_w, w_branch_conv,
           w_branch_attn, w_out, norm2_g, w_router, b_router, w_gate_up, b_gate_up, w_down, b_down):
    T, D = x2.shape
    attn_w = N_GROUPS * HEADS_PER_GROUP * HEAD_DIM
    cuts = np.cumsum([0, D, D, D, attn_w, attn_w, attn_w, D, D])
    order = [0, 1, 2, 6, 7, 3, 4, 5]
    w_in_r = jnp.concatenate([w_in[:, cuts[s]:cuts[s + 1]] for s in order], axis=1).astype(BF16)
    q_col = 5 * D // LANES
    k_col = q_col + attn_w // LANES
    v_col = k_col + attn_w // LANES

    proj = _inproj(x2, norm1_g.reshape(1, D), w_in_r, tm=512, tn=w_in_r.shape[1] // 2)

    gq2 = jnp.tile(q_norm_g.reshape(1, HEAD_DIM), (1, LANES // HEAD_DIM))
    gk2 = jnp.tile(k_norm_g.reshape(1, HEAD_DIM), (1, LANES // HEAD_DIM))
    o_attn = _attention(proj, gq2, gk2, _attn_bias_tables(rel_bias), B, S, q_col, k_col, v_col)

    wr_hi = w_router.T.astype(BF16)
    wr_lo = (w_router.T - wr_hi.astype(F32)).astype(BF16)
    x1, h2, idx_t, gates_t = _post(
        proj, o_attn, x2, conv_w, w_branch_conv.astype(BF16), w_branch_attn.astype(BF16),
        w_out.astype(BF16), norm2_g.reshape(1, D), wr_hi, wr_lo, b_router.reshape(N_EXPERTS, 1),
        tm=512, seq_len=S)

    bm = MOE_BM
    nblk = T * TOP_K // bm + N_EXPERTS
    nblk_pad = -(-nblk // LANES) * LANES
    dest, be, ends = _route(idx_t, bm, nblk_pad)
    ends = ends.reshape(LANES)
    nused = ends[N_EXPERTS - 1:N_EXPERTS] // bm
    tt = ROW_TILE
    dest3 = dest.T.reshape(T // tt, 1, tt * TOP_K)

    xs = _dispatch(ends, dest3, h2, nblk * bm, bm)
    ys = _experts(be.reshape(nblk_pad), nused, xs,
                  w_gate_up, b_gate_up.reshape(N_EXPERTS, 1, -1),
                  w_down, b_down.reshape(N_EXPERTS, 1, -1), bm)
    return _combine(dest3, x1, gates_t.T, ys)


def kernel(x, norm1_g, w_in, q_norm_g, k_norm_g, rel_bias, conv_w, w_branch_conv, w_branch_attn, w_out, norm2_g, w_router, b_router, w_gate_up, b_gate_up, w_down, b_down):
    B, S, D = x.shape
    x2 = x.reshape(B * S, D)
    for l in range(norm1_g.shape[0]):
        x2 = _layer(x2, B, S, norm1_g[l], w_in[l], q_norm_g[l], k_norm_g[l], rel_bias, conv_w[l],
                    w_branch_conv[l], w_branch_attn[l], w_out[l], norm2_g[l], w_router[l],
                    b_router[l], w_gate_up[l], b_gate_up[l], w_down[l], b_down[l])
    return x2.reshape(B, S, D)
```

```python
import functools
import math

import numpy as np
import jax
import jax.numpy as jnp
from jax import lax
from jax.experimental import pallas as pl
from jax.experimental.pallas import tpu as pltpu

F32 = jnp.float32
BF16 = jnp.bfloat16
I32 = jnp.int32

HEAD_DIM = 64
ATTN_GROUPS = ((128, 1), (512, 4), (2048, 16))
HEADS_PER_GROUP = 4
N_GROUPS = len(ATTN_GROUPS)
ATTN_BLK = 128
N_BUCKETS = 32
MAX_DISTANCE = 2048
CONV_K = 3
N_EXPERTS = 32
TOP_K = 4
SWIGLU_LIMIT = 7.0
SWIGLU_ALPHA = 1.702
EPS = 1e-6
MASK_VALUE = -1e30

LANES = 128
VMEM_LIMIT = 52 * 1024 * 1024

MOE_BM = 512
ROW_TILE = 512
ISSUE_GROUP = 8


def _rms(x, gain):
    return x * lax.rsqrt(jnp.mean(x * x, axis=-1, keepdims=True) + EPS) * gain


def _dot(a, b):
    return jnp.dot(a, b, preferred_element_type=F32)


def _dot_nt(a, b):
    return lax.dot_general(a, b, (((1,), (1,)), ((), ())), preferred_element_type=F32)


def _inproj_kernel(x_ref, g_ref, w_ref, o_ref):
    h = _rms(x_ref[...], g_ref[...]).astype(BF16)
    o_ref[...] = _dot(h, w_ref[...]).astype(BF16)


def _inproj(x2, g1, w_in_bf, tm, tn):
    T, D = x2.shape
    N = w_in_bf.shape[1]
    return pl.pallas_call(
        _inproj_kernel,
        grid=(N // tn, T // tm),
        in_specs=[
            pl.BlockSpec((tm, D), lambda j, i: (i, 0)),
            pl.BlockSpec((1, D), lambda j, i: (0, 0)),
            pl.BlockSpec((D, tn), lambda j, i: (0, j)),
        ],
        out_specs=pl.BlockSpec((tm, tn), lambda j, i: (i, j)),
        out_shape=jax.ShapeDtypeStruct((T, N), BF16),
        compiler_params=pltpu.CompilerParams(
            dimension_semantics=("arbitrary", "arbitrary"), vmem_limit_bytes=VMEM_LIMIT),
        name="inproj",
    )(x2, g1, w_in_bf)


def _t5_bucket_np(dist):
    max_exact = N_BUCKETS // 2
    d = np.maximum(dist.astype(np.float64), 1.0)
    large = max_exact + (np.log(d / max_exact) / math.log(MAX_DISTANCE / max_exact)
                         * (N_BUCKETS - max_exact)).astype(np.int32)
    large = np.minimum(large, N_BUCKETS - 1)
    return np.where(dist < max_exact, dist, large)


def _attn_bias_tables(rel_bias):
    blk = ATTN_BLK
    qi = np.arange(blk)[:, None]
    kj = np.arange(2 * blk)[None, :]
    delta = qi + blk - kj
    band = (delta >= 0) & (delta <= blk)
    first = band & (kj >= blk)
    tables = []
    for gi, (_, dil) in enumerate(ATTN_GROUPS):
        bucket = _t5_bucket_np(np.clip(delta, 0, blk) * dil)
        hs = slice(gi * HEADS_PER_GROUP, (gi + 1) * HEADS_PER_GROUP)
        onehot = (jnp.asarray(bucket, I32)[..., None] == jnp.arange(N_BUCKETS, dtype=I32)).astype(F32)
        bias = jnp.einsum('qkb,bh->hqk', onehot, rel_bias[:, hs].astype(F32),
                          precision=lax.Precision.HIGHEST)
        rest = jnp.where(band[None], bias, MASK_VALUE)
        frst = jnp.where(first[None], bias, MASK_VALUE)
        tables.append(jnp.stack([rest, frst], axis=1))
    return jnp.stack(tables, axis=0)


def _attn_kernel(q0, q1, q2, k0, k1, k2, v0, v1, v2, gq_ref, gk_ref, bias_ref, o_ref,
                 qa, qb, ks, va, vb, acc_s, m_s, l_s):
    S = q0.shape[0]
    pad = ks.shape[0] - S
    blk = ATTN_BLK
    hd = HEAD_DIM
    q_refs, k_refs, v_refs = (q0, q1, q2), (k0, k1, k2), (v0, v1, v2)
    head_a = lax.broadcasted_iota(I32, (1, LANES), 1) < hd
    same_head = (lax.broadcasted_iota(I32, (LANES, LANES), 0) // hd
                 == lax.broadcasted_iota(I32, (LANES, LANES), 1) // hd).astype(BF16)

    zeros = jnp.zeros((pad, LANES), F32)
    ks[0:pad, :] = zeros
    va[0:pad, :] = zeros
    vb[0:pad, :] = zeros

    def head_norm(x, gain):
        sq = x * x
        hi = sq.astype(BF16)
        lo = (sq - hi.astype(F32)).astype(BF16)
        ss = _dot(hi, same_head) + _dot(lo, same_head)
        return x * lax.rsqrt(ss * (1.0 / hd) + EPS) * gain

    for g, (window, dil) in enumerate(ATTN_GROUPS):
        span = blk * dil
        nblk = S // span
        stride = dil if dil > 1 else None
        qn = head_norm(q_refs[g][...].astype(F32), gq_ref[...]) * (hd ** -0.5)
        qa[...] = jnp.where(head_a, qn, 0.0)
        qb[...] = jnp.where(head_a, 0.0, qn)
        ks[pad:pad + S, :] = head_norm(k_refs[g][...].astype(F32), gk_ref[...])
        v = v_refs[g][...].astype(F32)
        va[pad:pad + S, :] = jnp.where(head_a, v, 1.0)
        vb[pad:pad + S, :] = jnp.where(head_a, 1.0, v)

        def body(idx, carry, g=g, span=span, nblk=nblk, stride=stride):
            r = idx // nblk
            n = idx % nblk
            qstart = r + n * span
            rows = pl.ds(qstart, blk, stride=stride)
            win = pl.ds(pad + qstart - span, 2 * blk, stride=stride)
            first = jnp.where(n == 0, 1, 0)
            kw = ks[win, :].astype(BF16)
            res, ms = [], []
            for hh, (q_ref, v_ref) in enumerate(((qa, va), (qb, vb))):
                s = _dot_nt(q_ref[rows, :].astype(BF16), kw) + bias_ref[g, hh, first]
                m = jnp.max(s, axis=-1, keepdims=True)
                p = jnp.exp(s - m).astype(BF16)
                res.append(_dot(p, v_ref[win, :].astype(BF16)))
                ms.append(m)
            acc_s[g, rows, :] = jnp.where(head_a, res[0], res[1])
            l_s[g, rows, :] = pltpu.roll(jnp.where(head_a, res[1], res[0]), hd, axis=1)
            m_s[g, rows, :] = jnp.where(head_a, ms[0], ms[1])
            return carry

        lax.fori_loop(0, S // blk, body, 0, unroll=8)

    m_all = jnp.maximum(jnp.maximum(m_s[0], m_s[1]), m_s[2])
    num = jnp.zeros((S, LANES), F32)
    den = jnp.zeros((S, LANES), F32)
    for g in range(N_GROUPS):
        w = jnp.exp(m_s[g] - m_all)
        num = num + w * acc_s[g]
        den = den + w * l_s[g]
    o_ref[...] = (num / den).astype(BF16)


def _attention(proj, gq2, gk2, bias_tab, B, S, q_col, k_col, v_col):
    T = proj.shape[0]
    n_pairs = HEADS_PER_GROUP // 2

    def col_spec(base, g):
        return pl.BlockSpec((S, LANES), lambda b, p, base=base, g=g: (b, base + n_pairs * g + p))

    in_specs = ([col_spec(q_col, g) for g in range(N_GROUPS)]
                + [col_spec(k_col, g) for g in range(N_GROUPS)]
                + [col_spec(v_col, g) for g in range(N_GROUPS)]
                + [pl.BlockSpec((1, LANES), lambda b, p: (0, 0)),
                   pl.BlockSpec((1, LANES), lambda b, p: (0, 0)),
                   pl.BlockSpec((N_GROUPS, 2, 2, ATTN_BLK, 2 * ATTN_BLK), lambda b, p: (0, p, 0, 0, 0))])
    pad = ATTN_BLK * max(d for _, d in ATTN_GROUPS)
    return pl.pallas_call(
        _attn_kernel,
        grid=(B, n_pairs),
        in_specs=in_specs,
        out_specs=pl.BlockSpec((S, LANES), lambda b, p: (b, p)),
        out_shape=jax.ShapeDtypeStruct((T, HEADS_PER_GROUP * HEAD_DIM), BF16),
        scratch_shapes=[
            pltpu.VMEM((S, LANES), F32),
            pltpu.VMEM((S, LANES), F32),
            pltpu.VMEM((pad + S, LANES), F32),
            pltpu.VMEM((pad + S, LANES), F32),
            pltpu.VMEM((pad + S, LANES), F32),
            pltpu.VMEM((N_GROUPS, S, LANES), F32),
            pltpu.VMEM((N_GROUPS, S, LANES), F32),
            pltpu.VMEM((N_GROUPS, S, LANES), F32),
        ],
        compiler_params=pltpu.CompilerParams(
            dimension_semantics=("arbitrary", "arbitrary"), vmem_limit_bytes=VMEM_LIMIT),
        name="attn",
    )(*([proj] * 9), gq2, gk2, bias_tab)


def _post_kernel(cb_ref, cc_ref, cx_ref, gc_ref, ga_ref, ccp_ref, cxp_ref, o_ref, x_ref, cw_ref,
                 wbc_ref, wba_ref, wout_ref, g2_ref, wrh_ref, wrl_ref, br_ref,
                 x1_ref, h2_ref, idx_ref, gate_ref, *, seq_len):
    i = pl.program_id(0)
    tm = x_ref.shape[0]
    u = cc_ref[...].astype(F32) * cx_ref[...].astype(F32)
    prev = ccp_ref[...].astype(F32) * cxp_ref[...].astype(F32)
    prev = jnp.where((i * tm) % seq_len == 0, 0.0, prev)
    last = prev.shape[0] - 1
    p1 = prev[last:last + 1, :]
    p2 = prev[last - 1:last, :]
    rows = lax.broadcasted_iota(I32, (tm, 1), 0)
    u1 = jnp.where(rows == 0, p1, pltpu.roll(u, 1, axis=0))
    u2 = jnp.where(rows == 0, p2, jnp.where(rows == 1, p1, pltpu.roll(u, 2, axis=0)))
    conv = cw_ref[0:1, :] * u2 + cw_ref[1:2, :] * u1 + cw_ref[2:3, :] * u
    y_conv = _dot((cb_ref[...].astype(F32) * conv).astype(BF16), wbc_ref[...])
    y_attn = _dot(o_ref[...], wba_ref[...])
    merged = (jax.nn.sigmoid(gc_ref[...].astype(F32)) * y_conv
              + jax.nn.sigmoid(ga_ref[...].astype(F32)) * y_attn)
    x1 = x_ref[...] + _dot(merged.astype(BF16), wout_ref[...])
    x1_ref[...] = x1
    h2 = _rms(x1, g2_ref[...])
    h2_ref[...] = h2

    h_hi = h2.astype(BF16)
    h_lo = (h2 - h_hi.astype(F32)).astype(BF16)
    logits = (_dot_nt(wrh_ref[...], h_hi) + _dot_nt(wrh_ref[...], h_lo)
              + _dot_nt(wrl_ref[...], h_hi)) + br_ref[...]
    e_iota = lax.broadcasted_iota(I32, logits.shape, 0)
    work = logits
    vals, idxs = [], []
    for _ in range(TOP_K):
        mx = jnp.max(work, axis=0, keepdims=True)
        ix = jnp.min(jnp.where(work == mx, e_iota, N_EXPERTS), axis=0, keepdims=True)
        vals.append(mx)
        idxs.append(ix)
        work = jnp.where(e_iota == ix, -jnp.inf, work)
    ex = [jnp.exp(v - vals[0]) for v in vals]
    tot = ex[0] + ex[1] + ex[2] + ex[3]
    idx_ref[...] = jnp.concatenate(idxs, axis=0)
    gate_ref[...] = jnp.concatenate([e / tot for e in ex], axis=0)


def _post(proj, o_attn, x2, conv_w, wbc, wba, wout, g2, wr_hi, wr_lo, br, tm, seq_len):
    T, D = x2.shape
    prev_rows = 16
    kern = functools.partial(_post_kernel, seq_len=seq_len)

    def colblk(c):
        return pl.BlockSpec((tm, D), lambda i, c=c: (i, c))

    def prevblk(c):
        return pl.BlockSpec((prev_rows, D),
                            lambda i, c=c: (jnp.maximum(i * (tm // prev_rows) - 1, 0), c))

    def whole(a):
        return pl.BlockSpec(a.shape, lambda i, nd=a.ndim: (0,) * nd)

    return pl.pallas_call(
        kern,
        grid=(T // tm,),
        in_specs=[colblk(0), colblk(1), colblk(2), colblk(3), colblk(4), prevblk(1), prevblk(2),
                  pl.BlockSpec((tm, o_attn.shape[1]), lambda i: (i, 0)),
                  pl.BlockSpec((tm, D), lambda i: (i, 0)),
                  whole(conv_w), whole(wbc), whole(wba), whole(wout), whole(g2),
                  whole(wr_hi), whole(wr_lo), whole(br)],
        out_specs=[pl.BlockSpec((tm, D), lambda i: (i, 0)),
                   pl.BlockSpec((tm, D), lambda i: (i, 0)),
                   pl.BlockSpec((TOP_K, tm), lambda i: (0, i)),
                   pl.BlockSpec((TOP_K, tm), lambda i: (0, i))],
        out_shape=[jax.ShapeDtypeStruct((T, D), F32),
                   jax.ShapeDtypeStruct((T, D), F32),
                   jax.ShapeDtypeStruct((TOP_K, T), I32),
                   jax.ShapeDtypeStruct((TOP_K, T), F32)],
        compiler_params=pltpu.CompilerParams(
            dimension_semantics=("arbitrary",), vmem_limit_bytes=VMEM_LIMIT),
        name="post",
    )(proj, proj, proj, proj, proj, proj, proj, o_attn, x2, conv_w, wbc, wba, wout, g2,
      wr_hi, wr_lo, br)


def _route_kernel(idx_ref, dest_ref, be_ref, ends_ref, rank_s, *, bm, chunk):
    T = idx_ref.shape[1]
    e_iota = lax.broadcasted_iota(I32, (N_EXPERTS, chunk), 0)
    upper = (lax.broadcasted_iota(I32, (chunk, chunk), 0)
             < lax.broadcasted_iota(I32, (chunk, chunk), 1)).astype(BF16)

    def onehots(c):
        off = pl.multiple_of(c * chunk, chunk)
        idc = idx_ref[:, pl.ds(off, chunk)]
        return off, [e_iota == idc[k:k + 1, :] for k in range(TOP_K)]

    def pass1(c, carry):
        off, oh = onehots(c)
        member = oh[0].astype(F32) + oh[1].astype(F32) + oh[2].astype(F32) + oh[3].astype(F32)
        before = _dot(member.astype(BF16), upper) + carry
        ranks = [jnp.sum(jnp.where(o, before, 0.0), axis=0, keepdims=True) for o in oh]
        rank_s[:, pl.ds(off, chunk)] = jnp.concatenate(ranks, axis=0)
        return carry + jnp.sum(member, axis=1, keepdims=True)

    counts = lax.fori_loop(0, T // chunk, pass1, jnp.zeros((N_EXPERTS, 1), F32))

    nb = jnp.floor((counts + (bm - 1)) * (1.0 / bm))
    lower = (lax.broadcasted_iota(I32, (N_EXPERTS, N_EXPERTS), 1)
             < lax.broadcasted_iota(I32, (N_EXPERTS, N_EXPERTS), 0)).astype(BF16)
    boff = _dot(lower, jnp.broadcast_to(nb, (N_EXPERTS, LANES)).astype(BF16))[:, 0:1]
    poff = boff * bm
    bend = boff + nb

    def pass2(c, carry):
        off, oh = onehots(c)
        offs = [jnp.sum(jnp.where(o, poff, 0.0), axis=0, keepdims=True) for o in oh]
        dest = jnp.concatenate(offs, axis=0) + rank_s[:, pl.ds(off, chunk)]
        dest_ref[:, pl.ds(off, chunk)] = dest.astype(I32)
        return carry

    lax.fori_loop(0, T // chunk, pass2, 0)

    nbp = be_ref.shape[1]
    b_iota = lax.broadcasted_iota(I32, (N_EXPERTS, nbp), 1).astype(F32)
    be = jnp.sum((bend <= b_iota).astype(F32), axis=0, keepdims=True)
    be_ref[...] = jnp.minimum(be, N_EXPERTS - 1).astype(I32)
    diag = (lax.broadcasted_iota(I32, (N_EXPERTS, LANES), 0)
            == lax.broadcasted_iota(I32, (N_EXPERTS, LANES), 1))
    ends_ref[...] = jnp.sum(jnp.where(diag, bend * bm, 0.0), axis=0, keepdims=True).astype(I32)


def _route(idx_t, bm, nblk_pad):
    K, T = idx_t.shape
    kern = functools.partial(_route_kernel, bm=bm, chunk=512)
    return pl.pallas_call(
        kern,
        out_shape=[jax.ShapeDtypeStruct((K, T), I32),
                   jax.ShapeDtypeStruct((1, nblk_pad), I32),
                   jax.ShapeDtypeStruct((1, LANES), I32)],
        scratch_shapes=[pltpu.VMEM((K, T), F32)],
        compiler_params=pltpu.CompilerParams(vmem_limit_bytes=VMEM_LIMIT),
        name="route",
    )(idx_t)


def _row_copy(src_ref, src_row, dst_ref, dst_row, sem):
    return pltpu.make_async_copy(src_ref.at[pl.ds(src_row, 1), :], dst_ref.at[pl.ds(dst_row, 1), :], sem)


def _dispatch_kernel(ends_ref, dest_ref, h_ref, xs_ref, zbuf, sem, zsem):
    tt = h_ref.shape[0]
    bm = zbuf.shape[0]

    @pl.when(pl.program_id(0) == 0)
    def _():
        zbuf[...] = jnp.zeros(zbuf.shape, F32)
        fills = []
        for e in range(N_EXPERTS):
            end = ends_ref[e]
            start = ends_ref[e - 1] if e > 0 else 0
            fills.append((end > start,
                          pltpu.make_async_copy(
                              zbuf, xs_ref.at[pl.ds(pl.multiple_of(end - bm, bm), bm), :], zsem)))
        for nonempty, fill in fills:
            pl.when(nonempty)(fill.start)
        for nonempty, fill in fills:
            pl.when(nonempty)(fill.wait)

        def tail(b):
            return pltpu.make_async_copy(
                zbuf, xs_ref.at[pl.ds(pl.multiple_of(b * bm, bm), bm), :], zsem)

        first_unused = ends_ref[N_EXPERTS - 1] // bm
        n_blocks = xs_ref.shape[0] // bm
        lax.fori_loop(first_unused, n_blocks, lambda b, c: (tail(b).start(), c)[1], 0)
        lax.fori_loop(first_unused, n_blocks, lambda b, c: (tail(b).wait(), c)[1], 0)

    def issue(i, carry):
        base = pl.multiple_of(i * ISSUE_GROUP, ISSUE_GROUP)
        slot0 = pl.multiple_of(i * (ISSUE_GROUP * TOP_K), ISSUE_GROUP * TOP_K)
        for j in range(ISSUE_GROUP):
            for k in range(TOP_K):
                _row_copy(h_ref, base + j, xs_ref, dest_ref[0, slot0 + (j * TOP_K + k)],
                          sem).start(priority=k % 2)
        return carry

    lax.fori_loop(0, tt // ISSUE_GROUP, issue, 0)

    for k in range(TOP_K):
        pltpu.make_async_copy(h_ref, xs_ref.at[pl.ds(0, tt), :], sem).wait()


def _dispatch(ends, dest3, h2, n_rows, bm):
    nt, _, slots = dest3.shape
    tt = slots // TOP_K
    T, D = h2.shape
    grid_spec = pltpu.PrefetchScalarGridSpec(
        num_scalar_prefetch=1,
        grid=(nt,),
        in_specs=[pl.BlockSpec((None, 1, slots), lambda i, ends_r: (i, 0, 0), memory_space=pltpu.SMEM),
                  pl.BlockSpec((tt, D), lambda i, ends_r: (i, 0))],
        out_specs=pl.BlockSpec(memory_space=pl.ANY),
        scratch_shapes=[pltpu.VMEM((bm, D), F32), pltpu.SemaphoreType.DMA(()),
                        pltpu.SemaphoreType.DMA(())],
    )
    return pl.pallas_call(
        _dispatch_kernel,
        grid_spec=grid_spec,
        out_shape=jax.ShapeDtypeStruct((n_rows, D), F32),
        compiler_params=pltpu.CompilerParams(
            dimension_semantics=("arbitrary",), vmem_limit_bytes=VMEM_LIMIT),
        name="dispatch",
    )(ends, dest3, h2)


def _expert_kernel(be_ref, nu_ref, xs_ref, wgu_ref, bgu_ref, wd_ref, bd_ref, ys_ref,
                   wgu_s, wd_s, *, chunk):
    b = pl.program_id(0)
    d_exp = wd_ref.shape[0]
    active = b < nu_ref[0]
    new_expert = jnp.logical_or(b == 0, be_ref[b] != be_ref[jnp.maximum(b - 1, 0)])

    @pl.when(jnp.logical_and(active, new_expert))
    def _():
        wgu_s[...] = wgu_ref[...].astype(BF16)
        wd_s[...] = wd_ref[...].astype(BF16)

    @pl.when(active)
    def _():
        x = xs_ref[...].astype(BF16)
        y = jnp.zeros(ys_ref.shape, F32)
        for c in range(d_exp // chunk):
            lo = c * chunk
            gate = _dot(x, wgu_s[:, lo:lo + chunk]) + bgu_ref[:, lo:lo + chunk]
            up = _dot(x, wgu_s[:, d_exp + lo:d_exp + lo + chunk]) + bgu_ref[:, d_exp + lo:d_exp + lo + chunk]
            gate = jnp.minimum(gate, SWIGLU_LIMIT)
            up = jnp.clip(up, -SWIGLU_LIMIT, SWIGLU_LIMIT)
            act = (up + 1.0) * (gate * jax.nn.sigmoid(SWIGLU_ALPHA * gate))
            y = y + _dot(act.astype(BF16), wd_s[lo:lo + chunk, :])
        ys_ref[...] = y + bd_ref[...]

    @pl.when(jnp.logical_not(active))
    def _():
        ys_ref[...] = jnp.zeros(ys_ref.shape, F32)


def _experts(be, nused, xs, wgu, bgu, wd, bd, bm):
    P, D = xs.shape
    E, _, two_de = wgu.shape
    d_exp = wd.shape[1]
    kern = functools.partial(_expert_kernel, chunk=512)

    def rows(b, be_r, nu_r):
        return (jnp.minimum(b, nu_r[0] - 1), 0)

    def per_expert(b, be_r, nu_r):
        return (be_r[b], 0, 0)

    grid_spec = pltpu.PrefetchScalarGridSpec(
        num_scalar_prefetch=2,
        grid=(P // bm,),
        in_specs=[pl.BlockSpec((bm, D), rows),
                  pl.BlockSpec((None, D, two_de), per_expert),
                  pl.BlockSpec((None, 1, two_de), per_expert),
                  pl.BlockSpec((None, d_exp, D), per_expert),
                  pl.BlockSpec((None, 1, D), per_expert)],
        out_specs=pl.BlockSpec((bm, D), lambda b, be_r, nu_r: (b, 0)),
        scratch_shapes=[pltpu.VMEM((D, two_de), BF16), pltpu.VMEM((d_exp, D), BF16)],
    )
    return pl.pallas_call(
        kern,
        grid_spec=grid_spec,
        out_shape=jax.ShapeDtypeStruct((P, D), F32),
        compiler_params=pltpu.CompilerParams(
            dimension_semantics=("arbitrary",), vmem_limit_bytes=VMEM_LIMIT),
        name="experts",
    )(be, nused, xs, wgu, bgu, wd, bd)


def _combine_kernel(dest_ref, x1_ref, gates_ref, ys_ref, out_ref, buf, sem):
    tt = x1_ref.shape[0]

    def issue(i, carry):
        base = pl.multiple_of(i * ISSUE_GROUP, ISSUE_GROUP)
        slot0 = pl.multiple_of(i * (ISSUE_GROUP * TOP_K), ISSUE_GROUP * TOP_K)
        for j in range(ISSUE_GROUP):
            for k in range(TOP_K):
                _row_copy(ys_ref, dest_ref[0, slot0 + (j * TOP_K + k)], buf.at[k], base + j,
                          sem).start(priority=k % 2)
        return carry

    lax.fori_loop(0, tt // ISSUE_GROUP, issue, 0)

    for k in range(TOP_K):
        pltpu.make_async_copy(ys_ref.at[pl.ds(0, tt), :], buf.at[k], sem).wait()

    acc = x1_ref[...]
    for k in range(TOP_K):
        acc = acc + gates_ref[:, k:k + 1] * buf[k]
    out_ref[...] = acc


def _combine(dest3, x1, gates, ys):
    nt, _, slots = dest3.shape
    K = TOP_K
    tt = slots // K
    T, D = x1.shape
    return pl.pallas_call(
        _combine_kernel,
        grid=(nt,),
        in_specs=[pl.BlockSpec((None, 1, slots), lambda i: (i, 0, 0), memory_space=pltpu.SMEM),
                  pl.BlockSpec((tt, D), lambda i: (i, 0)),
                  pl.BlockSpec((tt, K), lambda i: (i, 0)),
                  pl.BlockSpec(memory_space=pl.ANY)],
        out_specs=pl.BlockSpec((tt, D), lambda i: (i, 0)),
        out_shape=jax.ShapeDtypeStruct((T, D), F32),
        scratch_shapes=[pltpu.VMEM((K, tt, D), F32), pltpu.SemaphoreType.DMA(())],
        compiler_params=pltpu.CompilerParams(
            dimension_semantics=("arbitrary",), vmem_limit_bytes=VMEM_LIMIT),
        name="combine",
    )(dest3, x1, gates, ys)


def _layer(x2, B, S, norm1_g, w_in, q_norm_g, k_norm_g, rel_bias, conv_w, w_branch_conv,
           w_branch_attn, w_out, norm2_g, w_router, b_router, w_gate_up, b_gate_up, w_down, b_down):
    T, D = x2.shape
    attn_w = N_GROUPS * HEADS_PER_GROUP * HEAD_DIM
    cuts = np.cumsum([0, D, D, D, attn_w, attn_w, attn_w, D, D])
    order = [0, 1, 2, 6, 7, 3, 4, 5]
    w_in_r = jnp.concatenate([w_in[:, cuts[s]:cuts[s + 1]] for s in order], axis=1).astype(BF16)
    q_col = 5 * D // LANES
    k_col = q_col + attn_w // LANES
    v_col = k_col + attn_w // LANES

    proj = _inproj(x2, norm1_g.reshape(1, D), w_in_r, tm=512, tn=w_in_r.shape[1] // 2)

    gq2 = jnp.tile(q_norm_g.reshape(1, HEAD_DIM), (1, LANES // HEAD_DIM))
    gk2 = jnp.tile(k_norm_g.reshape(1, HEAD_DIM), (1, LANES // HEAD_DIM))
    o_attn = _attention(proj, gq2, gk2, _attn_bias_tables(rel_bias), B, S, q_col, k_col, v_col)

    wr_hi = w_router.T.astype(BF16)
    wr_lo = (w_router.T - wr_hi.astype(F32)).astype(BF16)
    x1, h2, idx_t, gates_t = _post(
        proj, o_attn, x2, conv_w, w_branch_conv.astype(BF16), w_branch_attn.astype(BF16),
        w_out.astype(BF16), norm2_g.reshape(1, D), wr_hi, wr_lo, b_router.reshape(N_EXPERTS, 1),
        tm=512, seq_len=S)

    bm = MOE_BM
    nblk = T * TOP_K // bm + N_EXPERTS
    nblk_pad = -(-nblk // LANES) * LANES
    dest, be, ends = _route(idx_t, bm, nblk_pad)
    ends = ends.reshape(LANES)
    nused = ends[N_EXPERTS - 1:N_EXPERTS] // bm
    tt = ROW_TILE
    dest3 = dest.T.reshape(T // tt, 1, tt * TOP_K)

    xs = _dispatch(ends, dest3, h2, nblk * bm, bm)
    ys = _experts(be.reshape(nblk_pad), nused, xs,
                  w_gate_up, b_gate_up.reshape(N_EXPERTS, 1, -1),
                  w_down, b_down.reshape(N_EXPERTS, 1, -1), bm)
    return _combine(dest3, x1, gates_t.T, ys)


def kernel(x, norm1_g, w_in, q_norm_g, k_norm_g, rel_bias, conv_w, w_branch_conv, w_branch_attn, w_out, norm2_g, w_router, b_router, w_gate_up, b_gate_up, w_down, b_down):
    B, S, D = x.shape
    x2 = x.reshape(B * S, D)
    for l in range(norm1_g.shape[0]):
        x2 = _layer(x2, B, S, norm1_g[l], w_in[l], q_norm_g[l], k_norm_g[l], rel_bias, conv_w[l],
                    w_branch_conv[l], w_branch_attn[l], w_out[l], norm2_g[l], w_router[l],
                    b_router[l], w_gate_up[l], b_gate_up[l], w_down[l], b_down[l])
    return x2.reshape(B, S, D)
```
